```python
import math
import jax, jax.numpy as jnp
from jax import lax
import numpy as np

D_MODEL = 2048
BATCH = 4
SEQ = 2048
DEPTH = 4
DEC_BATCH = 8
DEC_SEQ = 8
PAST_LEN = 16384
PAGE_SIZE = 128

N_PAGES = PAST_LEN // PAGE_SIZE
N_PHYS_PAGES = DEC_BATCH * N_PAGES + (DEC_BATCH * N_PAGES) // 4 + 1

N_MIXERS = 4
EPS = 1e-6
SB_HEADS = 16
SB_HEAD_DIM = D_MODEL // SB_HEADS
SB_BLOCK = 128
SB_BIAS_INIT = -6.0
S5_GROUP_CH = 16
S5_GROUPS = D_MODEL // S5_GROUP_CH
S5_STATE = 64
S5_CHUNK = 128
CONV_W = 3
GLA_HEADS = 4
GLA_DK = D_MODEL // 2 // GLA_HEADS
GLA_DV = D_MODEL // GLA_HEADS
GLA_LOWRANK = 16
GLA_TAU = 16.0
GLA_CHUNK = 32
N_MEM = 256
X_HEADS = 4
X_HEAD_DIM = 128
D_FF = 256 * (-(-8 * D_MODEL // (3 * 256)))

kernel_name = 'hybrid_sb_s5_conv_gla_decode_step'


def _rms_norm(x, g):
    xf = x.astype(jnp.float32)
    y = xf * lax.rsqrt(jnp.mean(xf * xf, axis=-1, keepdims=True) + EPS)
    return (y * g.astype(jnp.float32)).astype(x.dtype)


def _sb_block(carry, q, qpos, k, v, kpos, bias):
    acc, out = carry
    z = jnp.einsum('bhqd,bkhd->bhqk', q, k.astype(jnp.float32)) + bias[None, :, None, None]
    mask = kpos[None, :] < qpos[:, None]
    log_keep = jnp.where(mask, jax.nn.log_sigmoid(-z), 0.0)
    after = lax.cumsum(log_keep, axis=3, reverse=True) - log_keep
    log_w = jax.nn.log_sigmoid(z) + after + acc[..., None]
    w = jnp.where(mask, jnp.exp(log_w), 0.0)
    out = out + jnp.einsum('bhqk,bkhd->bhqd', w, v.astype(jnp.float32))
    return (acc + jnp.sum(log_keep, axis=3), out)


def _sb_init(bt, t):
    return (jnp.zeros((bt, SB_HEADS, t), jnp.float32),
            jnp.zeros((bt, SB_HEADS, t, SB_HEAD_DIM), jnp.float32))


def _sb_attend_prompt(q, k, v, bias):
    bt, t = q.shape[0], q.shape[1]
    nb = t // SB_BLOCK
    qf = (q.astype(jnp.float32) * SB_HEAD_DIM ** -0.5).transpose(0, 2, 1, 3)
    offs = jnp.arange(SB_BLOCK)

    def per_block(i):
        qb = lax.dynamic_slice_in_dim(qf, i * SB_BLOCK, SB_BLOCK, axis=2)
        qpos = i * SB_BLOCK + offs

        def body(n, carry):
            j = i - n
            kb = lax.dynamic_slice_in_dim(k, j * SB_BLOCK, SB_BLOCK, axis=1)
            vb = lax.dynamic_slice_in_dim(v, j * SB_BLOCK, SB_BLOCK, axis=1)
            return _sb_block(carry, qb, qpos, kb, vb, j * SB_BLOCK + offs, bias)

        return lax.fori_loop(0, i + 1, body, _sb_init(bt, SB_BLOCK))[1]

    o = lax.map(per_block, jnp.arange(nb))
    return o.transpose(1, 0, 3, 2, 4).reshape(bt, t, SB_HEADS, SB_HEAD_DIM)


def _sb_attend_sample(q, k, v, bias, cache_k, cache_v, page_table):
    bt, t = q.shape[0], q.shape[1]
    qf = (q.astype(jnp.float32) * SB_HEAD_DIM ** -0.5).transpose(0, 2, 1, 3)
    qpos = PAST_LEN + jnp.arange(t)
    carry = _sb_block(_sb_init(bt, t), qf, qpos, k, v, qpos, bias)
    offs = jnp.arange(PAGE_SIZE)

    def body(carry, p):
        phys = page_table[:, p]
        return _sb_block(carry, qf, qpos, cache_k[phys], cache_v[phys], p * PAGE_SIZE + offs, bias), None

    carry, _ = lax.scan(body, carry, jnp.arange(N_PAGES - 1, -1, -1))
    return carry[1].transpose(0, 2, 1, 3)


def _sb_mixer(h, w_qkv, w_out, bias, attend):
    bt, t, d = h.shape
    q, k, v = jnp.split(h @ w_qkv, 3, axis=-1)
    shp = (bt, t, SB_HEADS, SB_HEAD_DIM)
    q, k, v = q.reshape(shp), k.reshape(shp), v.reshape(shp)
    o = attend(q, k, v, bias.astype(jnp.float32))
    return o.reshape(bt, t, d).astype(h.dtype) @ w_out, (k, v)


def _s5_combine(e1, e2):
    a1r, a1i, b1r, b1i = e1
    a2r, a2i, b2r, b2i = e2
    return (a2r * a1r - a2i * a1i, a2r * a1i + a2i * a1r,
            a2r * b1r - a2i * b1i + b2r, a2r * b1i + a2i * b1r + b2i)


def _s5_mixer(h, a_re, a_im, log_dt, b_re, b_im, c_re, c_im, d_skip, w_glu, prev):
    bt, t, d = h.shape
    f32 = jnp.float32
    u = h.astype(f32).reshape(bt, t, S5_GROUPS, S5_GROUP_CH)
    dt = jnp.exp(log_dt.astype(f32))[:, None]
    ar, ai = a_re.astype(f32), a_im.astype(f32)
    mag = jnp.exp(ar * dt)
    abar_r, abar_i = mag * jnp.cos(ai * dt), mag * jnp.sin(ai * dt)
    xr, xi = abar_r - 1.0, abar_i
    den = ar * ar + ai * ai
    coef_r = (xr * ar + xi * ai) / den
    coef_i = (xi * ar - xr * ai) / den
    br, bi = b_re.astype(f32), b_im.astype(f32)
    bbar_r = coef_r[..., None] * br - coef_i[..., None] * bi
    bbar_i = coef_r[..., None] * bi + coef_i[..., None] * br
    cr, ci = c_re.astype(f32), c_im.astype(f32)
    c = S5_CHUNK if t % S5_CHUNK == 0 else t
    n = t // c
    uc = u.reshape(bt, n, c, S5_GROUPS, S5_GROUP_CH).transpose(1, 0, 2, 3, 4)
    if prev is None:
        s0 = (jnp.zeros((bt, S5_GROUPS, S5_STATE), f32), jnp.zeros((bt, S5_GROUPS, S5_STATE), f32))
    else:
        s0 = (prev[0].astype(f32), prev[1].astype(f32))

    def step(carry, ub):
        sr, si = carry
        bu_r = jnp.einsum('btgc,gpc->btgp', ub, bbar_r)
        bu_i = jnp.einsum('btgc,gpc->btgp', ub, bbar_i)
        bu_r = bu_r.at[:, 0].add(abar_r * sr - abar_i * si)
        bu_i = bu_i.at[:, 0].add(abar_r * si + abar_i * sr)
        shp = bu_r.shape
        _, _, st_r, st_i = lax.associative_scan(
            _s5_combine, (jnp.broadcast_to(abar_r, shp), jnp.broadcast_to(abar_i, shp), bu_r, bu_i), axis=1)
        y = jnp.einsum('btgp,gcp->btgc', st_r, cr) - jnp.einsum('btgp,gcp->btgc', st_i, ci)
        return (st_r[:, -1], st_i[:, -1]), y

    (sr, si), ys = lax.scan(step, s0, uc)
    y = ys.transpose(1, 0, 2, 3, 4).reshape(bt, t, d) + d_skip.astype(f32) * h.astype(f32)
    g = jax.nn.gelu(y).astype(h.dtype)
    o1, o2 = jnp.split(g @ w_glu, 2, axis=-1)
    return o1 * jax.nn.sigmoid(o2), (sr, si)


def _conv_mixer(h, w_in, w_conv, w_out, prev):
    bt, t, d = h.shape
    b_gate, c_gate, v = jnp.split(h @ w_in, 3, axis=-1)
    z = c_gate * v
    if prev is None:
        prev = jnp.zeros((bt, CONV_W - 1, d), z.dtype)
    zp = jnp.concatenate([prev.astype(z.dtype), z], axis=1)
    y = lax.conv_general_dilated(zp, w_conv[:, None, :].astype(zp.dtype), window_strides=(1,),
                                 padding='VALID', dimension_numbers=('NWC', 'WIO', 'NWC'),
                                 feature_group_count=d)
    return (b_gate * y) @ w_out, zp[:, -(CONV_W - 1):]


def _gla_mixer(h, w_in, w_g1, w_g2, b_g, norm_g, w_out, prev):
    bt, t, d = h.shape
    f32 = jnp.float32
    gk = GLA_HEADS * GLA_DK
    q, k, v, r = jnp.split(h @ w_in, [gk, 2 * gk, 2 * gk + d], axis=-1)
    glog = jax.nn.log_sigmoid(((h @ w_g1) @ w_g2 + b_g).astype(f32)) / GLA_TAU
    q = q.astype(f32).reshape(bt, t, GLA_HEADS, GLA_DK) * GLA_DK ** -0.5
    k = k.astype(f32).reshape(bt, t, GLA_HEADS, GLA_DK)
    v = v.astype(f32).reshape(bt, t, GLA_HEADS, GLA_DV)
    glog = glog.reshape(bt, t, GLA_HEADS, GLA_DK)
    c = GLA_CHUNK if t % GLA_CHUNK == 0 else t
    n = t // c

    def chunks(a):
        return a.reshape((bt, n, c) + a.shape[2:]).swapaxes(0, 1)

    s0 = jnp.zeros((bt, GLA_HEADS, GLA_DK, GLA_DV), f32) if prev is None else prev.astype(f32)
    mask = jnp.tril(jnp.ones((c, c), bool))[None, :, :, None, None]

    def step(s, blk):
        qc, kc, vc, gc = blk
        b = jnp.cumsum(gc, axis=1)
        o_inter = jnp.einsum('bihd,bhde->bihe', qc * jnp.exp(b), s)
        diff = b[:, :, None] - b[:, None, :]
        decay = jnp.exp(jnp.where(mask, diff, -jnp.inf))
        att = jnp.einsum('bihd,bjhd,bijhd->bhij', qc, kc, decay)
        o_intra = jnp.einsum('bhij,bjhe->bihe', att, vc)
        b_last = b[:, -1]
        s_new = jnp.exp(b_last)[..., None] * s + jnp.einsum(
            'bjhd,bjhe->bhde', kc * jnp.exp(b_last[:, None] - b), vc)
        return s_new, o_inter + o_intra

    s, os_ = lax.scan(step, s0, (chunks(q), chunks(k), chunks(v), chunks(glog)))
    o = os_.swapaxes(0, 1).reshape(bt, t, GLA_HEADS, GLA_DV)
    o = o * lax.rsqrt(jnp.mean(o * o, axis=-1, keepdims=True) + EPS) * norm_g.astype(f32)
    o = o.reshape(bt, t, d) * jax.nn.silu(r.astype(f32))
    return o.astype(h.dtype) @ w_out, s


def _cross_attn(h, mk, mv, w_q, w_o):
    bt, t, _ = h.shape
    q = (h @ w_q).reshape(bt, t, X_HEADS, X_HEAD_DIM).astype(jnp.float32)
    s = jnp.einsum('bthd,bmhd->bhtm', q, mk.astype(jnp.float32)) * X_HEAD_DIM ** -0.5
    a = jax.nn.softmax(s, axis=-1)
    o = jnp.einsum('bhtm,bmhd->bthd', a, mv.astype(jnp.float32)).reshape(bt, t, X_HEADS * X_HEAD_DIM)
    return o.astype(h.dtype) @ w_o


def _swiglu(h, w_in, w_out):
    g, u = jnp.split(h @ w_in, 2, axis=-1)
    return (jax.nn.silu(g) * u) @ w_out


def _trunk(x, mem_k, mem_v, sb_attend, s5_prev, conv_prev, gla_prev, p):
    new = {}
    for i in range(DEPTH):
        h = _rms_norm(x, p['norm_mix'][i])
        kind = i % N_MIXERS
        if kind == 0:
            out, new['sb'] = _sb_mixer(h, p['w_sb_qkv'], p['w_sb_out'], p['sb_bias'], sb_attend)
        elif kind == 1:
            out, new['s5'] = _s5_mixer(h, p['s5_a_re'], p['s5_a_im'], p['s5_log_dt'], p['s5_b_re'],
                                       p['s5_b_im'], p['s5_c_re'], p['s5_c_im'], p['s5_d'],
                                       p['w_s5_glu'], s5_prev)
        elif kind == 2:
            out, new['conv'] = _conv_mixer(h, p['w_conv_in'], p['w_conv'], p['w_conv_out'], conv_prev)
        else:
            out, new['gla'] = _gla_mixer(h, p['w_gla_in'], p['w_gla_g1'], p['w_gla_g2'], p['b_gla_g'],
                                         p['gla_norm'], p['w_gla_out'], gla_prev)
        x = x + out.astype(x.dtype)
        h = _rms_norm(x, p['norm_xattn'][i])
        x = x + _cross_attn(h, mem_k[i], mem_v[i], p['w_xq'][i], p['w_xo'][i]).astype(x.dtype)
        h = _rms_norm(x, p['norm_ffn'][i])
        x = x + _swiglu(h, p['w_ffn_in'][i], p['w_ffn_out'][i]).astype(x.dtype)
    return _rms_norm(x, p['norm_final']), new['sb'], new['s5'], new['conv'], new['gla']


def setup_inputs(seed: int = 0) -> dict:
    key = jax.random.key(seed)
    ks = iter(jax.random.split(key, 64))
    f32 = jnp.float32

    def nrm(shape, scale=1.0):
        return jax.random.normal(next(ks), shape, f32) * scale

    def gain(shape):
        return 1.0 + nrm(shape, 0.02)

    D = D_MODEL
    GK = GLA_HEADS * GLA_DK
    XW = X_HEADS * X_HEAD_DIM
    page_table = jax.random.permutation(next(ks), N_PHYS_PAGES)[: DEC_BATCH * N_PAGES]
    page_table = page_table.reshape(DEC_BATCH, N_PAGES).astype(jnp.int32)
    return {
        'x_prompt': nrm((BATCH, SEQ, D)),
        'x_sample': nrm((DEC_BATCH, DEC_SEQ, D)),
        'mem_prompt': nrm((BATCH, N_MEM, D)),
        'cache_sb_k': nrm((N_PHYS_PAGES, PAGE_SIZE, SB_HEADS, SB_HEAD_DIM)),
        'cache_sb_v': nrm((N_PHYS_PAGES, PAGE_SIZE, SB_HEADS, SB_HEAD_DIM)),
        'page_table': page_table,
        'state_s5_re': nrm((DEC_BATCH, S5_GROUPS, S5_STATE)),
        'state_s5_im': nrm((DEC_BATCH, S5_GROUPS, S5_STATE)),
        'state_conv': nrm((DEC_BATCH, CONV_W - 1, D)),
        'state_gla': nrm((DEC_BATCH, GLA_HEADS, GLA_DK, GLA_DV)),
        'cache_mem_k': nrm((DEPTH, DEC_BATCH, N_MEM, X_HEADS, X_HEAD_DIM)),
        'cache_mem_v': nrm((DEPTH, DEC_BATCH, N_MEM, X_HEADS, X_HEAD_DIM)),
        'norm_mix': gain((DEPTH, D)),
        'norm_xattn': gain((DEPTH, D)),
        'norm_mem': gain((DEPTH, D)),
        'norm_ffn': gain((DEPTH, D)),
        'norm_final': gain((D,)),
        'w_sb_qkv': nrm((D, 3 * D), D ** -0.5),
        'w_sb_out': nrm((D, D), D ** -0.5),
        'sb_bias': SB_BIAS_INIT + nrm((SB_HEADS,), 0.1),
        's5_a_re': -0.5 + nrm((S5_GROUPS, S5_STATE), 0.01),
        's5_a_im': math.pi * jnp.arange(S5_STATE, dtype=f32)[None, :] + nrm((S5_GROUPS, S5_STATE), 0.01),
        's5_log_dt': jax.random.uniform(next(ks), (S5_GROUPS,), f32, math.log(1e-3), math.log(1e-1)),
        's5_b_re': nrm((S5_GROUPS, S5_STATE, S5_GROUP_CH), (2 * S5_GROUP_CH) ** -0.5),
        's5_b_im': nrm((S5_GROUPS, S5_STATE, S5_GROUP_CH), (2 * S5_GROUP_CH) ** -0.5),
        's5_c_re': nrm((S5_GROUPS, S5_GROUP_CH, S5_STATE), (2 * S5_STATE) ** -0.5),
        's5_c_im': nrm((S5_GROUPS, S5_GROUP_CH, S5_STATE), (2 * S5_STATE) ** -0.5),
        's5_d': nrm((D,)),
        'w_s5_glu': nrm((D, 2 * D), D ** -0.5),
        'w_conv_in': nrm((D, 3 * D), D ** -0.5),
        'w_conv': nrm((CONV_W, D), CONV_W ** -0.5),
        'w_conv_out': nrm((D, D), D ** -0.5),
        'w_gla_in': nrm((D, 2 * GK + 2 * D), D ** -0.5),
        'w_gla_g1': nrm((D, GLA_LOWRANK), D ** -0.5),
        'w_gla_g2': nrm((GLA_LOWRANK, GK), GLA_LOWRANK ** -0.5),
        'b_gla_g': nrm((GK,), 0.01),
        'gla_norm': gain((GLA_DV,)),
        'w_gla_out': nrm((D, D), D ** -0.5),
        'w_xq': nrm((DEPTH, D, XW), D ** -0.5),
        'w_xk': nrm((DEPTH, D, XW), D ** -0.5),
        'w_xv': nrm((DEPTH, D, XW), D ** -0.5),
        'w_xo': nrm((DEPTH, XW, D), XW ** -0.5),
        'w_ffn_in': nrm((DEPTH, D, 2 * D_FF), D ** -0.5),
        'w_ffn_out': nrm((DEPTH, D_FF, D), D_FF ** -0.5),
    }


def reference(x_prompt, x_sample, mem_prompt, cache_sb_k, cache_sb_v, page_table,
              state_s5_re, state_s5_im, state_conv, state_gla, cache_mem_k, cache_mem_v,
              norm_mix, norm_xattn, norm_mem, norm_ffn, norm_final,
              w_sb_qkv, w_sb_out, sb_bias,
              s5_a_re, s5_a_im, s5_log_dt, s5_b_re, s5_b_im, s5_c_re, s5_c_im, s5_d, w_s5_glu,
              w_conv_in, w_conv, w_conv_out,
              w_gla_in, w_gla_g1, w_gla_g2, b_gla_g, gla_norm, w_gla_out,
              w_xq, w_xk, w_xv, w_xo,
              w_ffn_in, w_ffn_out):
    p = {
        'norm_mix': norm_mix, 'norm_xattn': norm_xattn, 'norm_ffn': norm_ffn, 'norm_final': norm_final,
        'w_sb_qkv': w_sb_qkv, 'w_sb_out': w_sb_out, 'sb_bias': sb_bias,
        's5_a_re': s5_a_re, 's5_a_im': s5_a_im, 's5_log_dt': s5_log_dt, 's5_b_re': s5_b_re,
        's5_b_im': s5_b_im, 's5_c_re': s5_c_re, 's5_c_im': s5_c_im, 's5_d': s5_d, 'w_s5_glu': w_s5_glu,
        'w_conv_in': w_conv_in, 'w_conv': w_conv, 'w_conv_out': w_conv_out,
        'w_gla_in': w_gla_in, 'w_gla_g1': w_gla_g1, 'w_gla_g2': w_gla_g2, 'b_gla_g': b_gla_g,
        'gla_norm': gla_norm, 'w_gla_out': w_gla_out,
        'w_xq': w_xq, 'w_xo': w_xo, 'w_ffn_in': w_ffn_in, 'w_ffn_out': w_ffn_out,
    }
    mf = mem_prompt.astype(jnp.float32)
    mn = mf * lax.rsqrt(jnp.mean(mf * mf, axis=-1, keepdims=True) + EPS)
    mn = (mn[None] * norm_mem.astype(jnp.float32)[:, None, None, :]).astype(mem_prompt.dtype)
    mshape = (DEPTH, mem_prompt.shape[0], N_MEM, X_HEADS, X_HEAD_DIM)
    mem_k_p = jnp.einsum('lbmd,lde->lbme', mn, w_xk).reshape(mshape)
    mem_v_p = jnp.einsum('lbmd,lde->lbme', mn, w_xv).reshape(mshape)

    y_p, (sbk_p, sbv_p), (s5r_p, s5i_p), conv_p, gla_p = _trunk(
        x_prompt, mem_k_p, mem_v_p, _sb_attend_prompt, None, None, None, p)

    def sample_attend(q, k, v, bias):
        return _sb_attend_sample(q, k, v, bias, cache_sb_k, cache_sb_v, page_table)

    y_s, (sbk_s, sbv_s), (s5r_s, s5i_s), conv_s, gla_s = _trunk(
        x_sample, cache_mem_k, cache_mem_v, sample_attend, (state_s5_re, state_s5_im),
        state_conv, state_gla, p)

    return (y_p, y_s, sbk_p, sbv_p, sbk_s, sbv_s, s5r_p, s5i_p, s5r_s, s5i_s,
            conv_p, conv_s, gla_p, gla_s, mem_k_p, mem_v_p)
```

```python
import functools
import math

import jax
import jax.numpy as jnp
from jax import lax
from jax.experimental import pallas as pl
from jax.experimental.pallas import tpu as pltpu

F32 = jnp.float32
BF16 = jnp.bfloat16
EPS = 1e-6
NEG_INF = float("-inf")

V7X_VMEM_BYTES = 64 * 1024 * 1024
V7X_SUBLANES = 8
V7X_LANES = 128

SB_HEADS = 16
SB_KEY_TILE = 128
S5_GROUP_CH = 16
S5_STATE = 64
S5_CH_BLOCK = 256
CONV_W = 3
GLA_HEADS = 4
GLA_TAU = 16.0
GLA_CHUNK = 32
GLA_SUB = 8
X_HEADS = 4
X_HEAD_DIM = 128

NT_DIMS = (((1,), (1,)), ((), ()))
TN_DIMS = (((0,), (0,)), ((), ()))


def _params(vmem_bytes, n_grid):
    limit = int(min(V7X_VMEM_BYTES - (6 << 20), max(vmem_bytes * 5 // 4 + (4 << 20), 16 << 20)))
    return pltpu.CompilerParams(dimension_semantics=("arbitrary",) * n_grid, vmem_limit_bytes=limit)


def _rms(x, g):
    return x * lax.rsqrt(jnp.mean(x * x, axis=-1, keepdims=True) + EPS) * g


def _softplus(z):
    return jnp.maximum(z, 0.0) + jnp.log1p(jnp.exp(-jnp.abs(z)))


def _dot(a, b):
    return jnp.dot(a, b, preferred_element_type=F32)


def _split_hi_lo(x):
    hi = x.astype(BF16)
    lo = (x - hi.astype(F32)).astype(BF16)
    return hi, lo


def _row_tile(m, cap):
    t = min(m, cap)
    while m % t:
        t //= 2
    return t


def _mm_body(*refs, norm, mode):
    refs = list(refs)
    x_ref = refs.pop(0)
    g_ref = refs.pop(0) if norm else None
    w_ref = refs.pop(0)
    w2_ref = refs.pop(0) if mode == "glu" else None
    res_ref = refs.pop(0) if mode in ("res", "glu") else None
    o_ref, lhs = refs

    @pl.when(pl.program_id(1) == 0)
    def _():
        x = x_ref[...]
        if norm:
            x = _rms(x, g_ref[...])
        lhs[...] = x.astype(BF16)

    a = lhs[...]
    y = _dot(a, w_ref[...].astype(BF16))
    if mode == "glu":
        y = y * jax.nn.sigmoid(_dot(a, w2_ref[...].astype(BF16)))
    if res_ref is not None:
        y = res_ref[...] + y
    o_ref[...] = y


def _mm(x, w, *, g=None, res=None, glu=False, tn=512, tm_cap=1024):
    m, k = x.shape
    n = w.shape[1] // (2 if glu else 1)
    tm = _row_tile(m, tm_cap)
    nj = n // tn
    mode = "glu" if glu else ("res" if res is not None else "plain")
    in_specs = [pl.BlockSpec((tm, k), lambda i, j: (i, 0))]
    args = [x]
    if g is not None:
        in_specs.append(pl.BlockSpec((1, k), lambda i, j: (0, 0)))
        args.append(g.reshape(1, k))
    in_specs.append(pl.BlockSpec((k, tn), lambda i, j: (0, j)))
    args.append(w)
    if glu:
        in_specs.append(pl.BlockSpec((k, tn), lambda i, j: (0, j + nj)))
        args.append(w)
    if res is not None:
        in_specs.append(pl.BlockSpec((tm, tn), lambda i, j: (i, j)))
        args.append(res)
    vmem = 2 * tm * k * 4 + tm * k * 2 + (2 if glu else 1) * 2 * k * tn * 4 + 4 * tm * tn * 4 + k * tn * 4
    return pl.pallas_call(
        functools.partial(_mm_body, norm=g is not None, mode=mode),
        grid=(m // tm, nj),
        in_specs=in_specs,
        out_specs=pl.BlockSpec((tm, tn), lambda i, j: (i, j)),
        out_shape=jax.ShapeDtypeStruct((m, n), F32),
        scratch_shapes=[pltpu.VMEM((tm, k), BF16)],
        compiler_params=_params(vmem, 2),
        name="mm_" + mode,
    )(*args)


def _ffn_body(x_ref, g_ref, wg_ref, wu_ref, wo_ref, o_ref, h_scr):
    @pl.when(pl.program_id(1) == 0)
    def _():
        x = x_ref[...]
        h_scr[...] = _rms(x, g_ref[...]).astype(BF16)
        o_ref[...] = x

    h = h_scr[...]
    gate = _dot(h, wg_ref[...].astype(BF16))
    up = _dot(h, wu_ref[...].astype(BF16))
    a = (jax.nn.silu(gate) * up).astype(BF16)
    o_ref[...] += _dot(a, wo_ref[...].astype(BF16))


def _ffn(x, g, w_in, w_out, layer, *, tf=256, tm_cap=1024):
    m, d = x.shape
    f = w_out.shape[1]
    tm = _row_tile(m, tm_cap)
    nf = f // tf
    vmem = 4 * tm * d * 4 + tm * d * 2 + 3 * 2 * d * tf * 4 + 3 * d * tf * 2 + 3 * tm * tf * 4
    return pl.pallas_call(
        _ffn_body,
        grid=(m // tm, nf),
        in_specs=[
            pl.BlockSpec((tm, d), lambda i, j: (i, 0)),
            pl.BlockSpec((1, d), lambda i, j: (0, 0)),
            pl.BlockSpec((None, d, tf), lambda i, j: (layer, 0, j)),
            pl.BlockSpec((None, d, tf), lambda i, j: (layer, 0, j + nf)),
            pl.BlockSpec((None, tf, d), lambda i, j: (layer, j, 0)),
        ],
        out_specs=pl.BlockSpec((tm, d), lambda i, j: (i, 0)),
        out_shape=jax.ShapeDtypeStruct((m, d), F32),
        scratch_shapes=[pltpu.VMEM((tm, d), BF16)],
        compiler_params=_params(vmem, 2),
        name="ffn",
    )(x, g.reshape(1, d), w_in, w_in, w_out)


def _xattn_body(x_ref, g_ref, wq_ref, wo_ref, mk_ref, mv_ref, o_ref, wq_s, wo_s):
    @pl.when(pl.program_id(0) == 0)
    def _():
        wq_s[...] = wq_ref[...].astype(BF16)
        wo_s[...] = wo_ref[...].astype(BF16)

    x = x_ref[...]
    h = _rms(x, g_ref[...]).astype(BF16)
    q = _dot(h, wq_s[...])
    mk = mk_ref[...].astype(BF16)
    mv = mv_ref[...].astype(BF16)
    heads = []
    for hh in range(X_HEADS):
        sl = slice(hh * X_HEAD_DIM, (hh + 1) * X_HEAD_DIM)
        s = lax.dot_general(q[:, sl].astype(BF16), mk[:, sl], NT_DIMS, preferred_element_type=F32)
        s = s * X_HEAD_DIM ** -0.5
        e = jnp.exp(s - jnp.max(s, axis=-1, keepdims=True))
        a = e / jnp.sum(e, axis=-1, keepdims=True)
        heads.append(_dot(a.astype(BF16), mv[:, sl]))
    o = jnp.concatenate(heads, axis=1).astype(BF16)
    o_ref[...] = x + _dot(o, wo_s[...])


def _xattn(x, g, w_q, w_o, mem_k, mem_v, layer, rows_per_batch, *, tm_cap=512):
    m, d = x.shape
    xw = w_q.shape[-1]
    n_mem = mem_k.shape[2]
    tm = _row_tile(rows_per_batch, tm_cap)
    tiles_per_batch = rows_per_batch // tm
    vmem = 4 * tm * d * 4 + 2 * 2 * d * xw * 4 + 2 * d * xw * 2 + 8 * n_mem * xw * 4 + 6 * tm * xw * 4
    mem_spec = pl.BlockSpec((None, None, n_mem, xw), lambda i: (layer, i // tiles_per_batch, 0, 0))
    return pl.pallas_call(
        _xattn_body,
        grid=(m // tm,),
        in_specs=[
            pl.BlockSpec((tm, d), lambda i: (i, 0)),
            pl.BlockSpec((1, d), lambda i: (0, 0)),
            pl.BlockSpec((None, d, xw), lambda i: (layer, 0, 0)),
            pl.BlockSpec((None, xw, d), lambda i: (layer, 0, 0)),
            mem_spec,
            mem_spec,
        ],
        out_specs=pl.BlockSpec((tm, d), lambda i: (i, 0)),
        out_shape=jax.ShapeDtypeStruct((m, d), F32),
        scratch_shapes=[pltpu.VMEM((d, xw), BF16), pltpu.VMEM((xw, d), BF16)],
        compiler_params=_params(vmem, 1),
        name="xattn",
    )(x, g.reshape(1, d), w_q, w_o, mem_k, mem_v)


def _memkv_body(m_ref, g_ref, wk_ref, wv_ref, k_ref, v_ref):
    mn = _rms(m_ref[...], g_ref[...]).astype(BF16)
    k_ref[...] = _dot(mn, wk_ref[...].astype(BF16))
    v_ref[...] = _dot(mn, wv_ref[...].astype(BF16))


def _memkv(mem, norm_mem, w_xk, w_xv):
    depth, d, xw = w_xk.shape
    m = mem.shape[0]
    out = jax.ShapeDtypeStruct((depth, m, xw), F32)
    vmem = 2 * m * d * 4 + m * d * 2 + 4 * d * xw * 4 + 2 * d * xw * 2 + 6 * m * xw * 4
    return pl.pallas_call(
        _memkv_body,
        grid=(depth,),
        in_specs=[
            pl.BlockSpec((m, d), lambda l: (0, 0)),
            pl.BlockSpec((None, 1, d), lambda l: (l, 0, 0)),
            pl.BlockSpec((None, d, xw), lambda l: (l, 0, 0)),
            pl.BlockSpec((None, d, xw), lambda l: (l, 0, 0)),
        ],
        out_specs=[pl.BlockSpec((None, m, xw), lambda l: (l, 0, 0))] * 2,
        out_shape=[out, out],
        compiler_params=_params(vmem, 1),
        name="memkv",
    )(mem, norm_mem.reshape(depth, 1, d), w_xk, w_xv)


def _rmsnorm_body(x_ref, g_ref, o_ref):
    o_ref[...] = _rms(x_ref[...], g_ref[...])


def _rmsnorm(x, g, *, tm_cap=512):
    m, d = x.shape
    tm = _row_tile(m, tm_cap)
    return pl.pallas_call(
        _rmsnorm_body,
        grid=(m // tm,),
        in_specs=[pl.BlockSpec((tm, d), lambda i: (i, 0)), pl.BlockSpec((1, d), lambda i: (0, 0))],
        out_specs=pl.BlockSpec((tm, d), lambda i: (i, 0)),
        out_shape=jax.ShapeDtypeStruct((m, d), F32),
        compiler_params=_params(6 * tm * d * 4, 1),
        name="rmsnorm",
    )(x, g.reshape(1, d))


def _suffix_matrix(n):
    r = lax.broadcasted_iota(jnp.int32, (n, n), 0)
    c = lax.broadcasted_iota(jnp.int32, (n, n), 1)
    return jnp.where(r >= c, 1.0, 0.0).astype(BF16)


def _sb_tile(z, mask, acc, u_mat):
    sp = _softplus(z)
    lk = -sp if mask is None else jnp.where(mask, -sp, 0.0)
    hi, lo = _split_hi_lo(lk)
    incl = _dot(hi, u_mat) + _dot(lo, u_mat)
    log_w = (z - sp) + (incl - lk) + acc
    w = jnp.exp(log_w)
    if mask is not None:
        w = jnp.where(mask, w, 0.0)
    return w, acc + incl[:, 0:1]


def _sbp_body(bias_ref, q_ref, k_ref, v_ref, o_ref, *, tq, tk, scale):
    h = pl.program_id(1)
    i = pl.program_id(2)
    bias = bias_ref[h]
    q = (q_ref[...] * scale).astype(BF16)
    qpos = i * tq + lax.broadcasted_iota(jnp.int32, (tq, tk), 0)
    koff = lax.broadcasted_iota(jnp.int32, (tq, tk), 1)
    u_mat = _suffix_matrix(tk)
    nk = (i + 1) * (tq // tk)

    def body(n, carry):
        acc, out = carry
        j = nk - 1 - n
        start = pl.multiple_of(j * tk, tk)
        kb = k_ref[pl.ds(start, tk), :].astype(BF16)
        vb = v_ref[pl.ds(start, tk), :].astype(BF16)
        z = lax.dot_general(q, kb, NT_DIMS, preferred_element_type=F32) + bias
        mask = (start + koff) < qpos
        w, acc = _sb_tile(z, mask, acc, u_mat)
        return acc, out + _dot(w.astype(BF16), vb)

    dh = q_ref.shape[1]
    _, out = lax.fori_loop(0, nk, body, (jnp.zeros((tq, 1), F32), jnp.zeros((tq, dh), F32)))
    o_ref[...] = out


def _sb_attend_prompt(qkv, bias, nb, t, *, tq=256):
    m, d3 = qkv.shape
    d = d3 // 3
    dh = d // SB_HEADS
    tq = min(tq, t)
    tk = min(SB_KEY_TILE, tq)
    nq = t // tq
    vmem = 4 * t * dh * 4 + 4 * tq * dh * 4 + 16 * tq * tk * 4
    return pl.pallas_call(
        functools.partial(_sbp_body, tq=tq, tk=tk, scale=dh ** -0.5),
        grid=(nb, SB_HEADS, nq),
        in_specs=[
            pl.BlockSpec(memory_space=pltpu.SMEM),
            pl.BlockSpec((tq, dh), lambda b, h, i: (b * nq + i, h)),
            pl.BlockSpec((t, dh), lambda b, h, i: (b, SB_HEADS + h)),
            pl.BlockSpec((t, dh), lambda b, h, i: (b, 2 * SB_HEADS + h)),
        ],
        out_specs=pl.BlockSpec((tq, dh), lambda b, h, i: (b * nq + i, h)),
        out_shape=jax.ShapeDtypeStruct((m, d), F32),
        compiler_params=_params(vmem, 3),
        name="sb_prompt",
    )(bias, qkv, qkv, qkv)


def _sbs_body(pt_ref, bias_ref, qkv_ref, kc_ref, vc_ref, o_ref, out_s, acc_s, *, t, dh, page, scale):
    del pt_ref
    s = pl.program_id(1)
    d = SB_HEADS * dh
    rows = SB_HEADS * t
    u_mat = _suffix_matrix(page)

    def attend(get_k, get_v, mask):
        zs = []
        for h in range(SB_HEADS):
            qh = (qkv_ref[:, h * dh:(h + 1) * dh] * scale).astype(BF16)
            zs.append(lax.dot_general(qh, get_k(h), NT_DIMS, preferred_element_type=F32) + bias_ref[h])
        z = jnp.concatenate(zs, axis=0)
        w, acc = _sb_tile(z, mask, acc_s[...], u_mat)
        acc_s[...] = acc
        for h in range(SB_HEADS):
            wh = w[h * t:(h + 1) * t, :].astype(BF16)
            out_s[h * t:(h + 1) * t, :] += _dot(wh, get_v(h))

    @pl.when(s == 0)
    def _():
        out_s[...] = jnp.zeros_like(out_s)
        acc_s[...] = jnp.zeros_like(acc_s)
        pad = jnp.zeros((page - t, dh), F32)
        key = lax.broadcasted_iota(jnp.int32, (rows, page), 1)
        qry = lax.broadcasted_iota(jnp.int32, (rows, page), 0) % t
        attend(
            lambda h: jnp.concatenate([qkv_ref[:, d + h * dh:d + (h + 1) * dh], pad], axis=0).astype(BF16),
            lambda h: jnp.concatenate([qkv_ref[:, 2 * d + h * dh:2 * d + (h + 1) * dh], pad], axis=0).astype(BF16),
            key < qry,
        )

    @pl.when(s > 0)
    def _():
        attend(
            lambda h: kc_ref[pl.ds(h, page, stride=SB_HEADS), :].astype(BF16),
            lambda h: vc_ref[pl.ds(h, page, stride=SB_HEADS), :].astype(BF16),
            None,
        )

    @pl.when(s == pl.num_programs(1) - 1)
    def _():
        for h in range(SB_HEADS):
            o_ref[:, h * dh:(h + 1) * dh] = out_s[h * t:(h + 1) * t, :]


def _sb_attend_sample(qkv, bias, cache_k, cache_v, page_table, nb, t):
    m, d3 = qkv.shape
    d = d3 // 3
    dh = d // SB_HEADS
    n_phys, page = cache_k.shape[0], cache_k.shape[1]
    n_pages = page_table.shape[1]
    ck = cache_k.reshape(n_phys, page * SB_HEADS, dh)
    cv = cache_v.reshape(n_phys, page * SB_HEADS, dh)

    def page_map(b, s, pt):
        return (pt[b, n_pages - jnp.maximum(s, 1)], 0, 0)

    cache_spec = pl.BlockSpec((None, page * SB_HEADS, dh), page_map)
    vmem = 4 * page * SB_HEADS * dh * 4 + 2 * t * d3 * 4 + 24 * SB_HEADS * t * page * 4
    return pl.pallas_call(
        functools.partial(_sbs_body, t=t, dh=dh, page=page, scale=dh ** -0.5),
        grid_spec=pltpu.PrefetchScalarGridSpec(
            num_scalar_prefetch=1,
            grid=(nb, n_pages + 1),
            in_specs=[
                pl.BlockSpec(memory_space=pltpu.SMEM),
                pl.BlockSpec((t, d3), lambda b, s, pt: (b, 0)),
                cache_spec,
                cache_spec,
            ],
            out_specs=pl.BlockSpec((t, d), lambda b, s, pt: (b, 0)),
            scratch_shapes=[pltpu.VMEM((SB_HEADS * t, dh), F32), pltpu.VMEM((SB_HEADS * t, 1), F32)],
        ),
        out_shape=jax.ShapeDtypeStruct((m, d), F32),
        compiler_params=_params(vmem, 2),
        name="sb_sample",
    )(page_table, bias, qkv, ck, cv)


def _cmul(ar, ai, br, bi):
    return ar * br - ai * bi, ar * bi + ai * br


def _gelu_tanh(x):
    return 0.5 * x * (1.0 + jnp.tanh(math.sqrt(2.0 / math.pi) * (x + 0.044715 * (x * x * x))))


def _s5_body(h_ref, ar_ref, ai_ref, br_ref, bi_ref, cr_ref, ci_ref, d_ref, s0r_ref, s0i_ref,
             y_ref, sr_ref, si_ref, xr_s, xi_s, pr_s, pi_s, cr_s, ci_s, *, tt):
    t = pl.program_id(2)
    n_state = xr_s.shape[1]

    @pl.when(t == 0)
    def _():
        ar, ai = ar_ref[...], ai_ref[...]
        pr, pi = ar, ai
        pr_s[0:1, :] = pr
        pi_s[0:1, :] = pi
        for r in range(1, V7X_SUBLANES):
            pr, pi = _cmul(pr, pi, ar, ai)
            pr_s[r:r + 1, :] = pr
            pi_s[r:r + 1, :] = pi
        cr_s[...] = s0r_ref[...]
        ci_s[...] = s0i_ref[...]

    h = h_ref[...]
    u = h.astype(BF16)
    xr_s[...] = _dot(u, br_ref[...].astype(BF16))
    xi_s[...] = _dot(u, bi_ref[...].astype(BF16))

    row = lax.broadcasted_iota(jnp.int32, (V7X_SUBLANES, n_state), 0)
    pw_r, pw_i = pr_s[...], pi_s[...]

    def block(blk, carry):
        c_r, c_i = carry
        rows = pl.ds(pl.multiple_of(blk * V7X_SUBLANES, V7X_SUBLANES), V7X_SUBLANES)
        x_r, x_i = xr_s[rows, :], xi_s[rows, :]
        for dist in (1, 2, 4):
            a_r, a_i = pw_r[dist - 1:dist, :], pw_i[dist - 1:dist, :]
            keep = row >= dist
            sh_r = jnp.where(keep, pltpu.roll(x_r, dist, 0), 0.0)
            sh_i = jnp.where(keep, pltpu.roll(x_i, dist, 0), 0.0)
            d_r, d_i = _cmul(a_r, a_i, sh_r, sh_i)
            x_r, x_i = x_r + d_r, x_i + d_i
        d_r, d_i = _cmul(pw_r, pw_i, c_r, c_i)
        x_r, x_i = x_r + d_r, x_i + d_i
        xr_s[rows, :] = x_r
        xi_s[rows, :] = x_i
        return x_r[V7X_SUBLANES - 1:, :], x_i[V7X_SUBLANES - 1:, :]

    c_r, c_i = lax.fori_loop(0, tt // V7X_SUBLANES, block, (cr_s[...], ci_s[...]))
    cr_s[...] = c_r
    ci_s[...] = c_i

    y = _dot(xr_s[...].astype(BF16), cr_ref[...].astype(BF16)) - _dot(xi_s[...].astype(BF16), ci_ref[...].astype(BF16))
    y_ref[...] = _gelu_tanh(y + d_ref[...] * h)

    @pl.when(t == pl.num_programs(2) - 1)
    def _():
        sr_ref[...] = c_r
        si_ref[...] = c_i


def _s5_discretize(a_re, a_im, log_dt, b_re, b_im, c_re, c_im):
    g, p = a_re.shape
    gb = S5_CH_BLOCK // S5_GROUP_CH
    nblk = g // gb
    dt = jnp.exp(log_dt)[:, None]
    mag = jnp.exp(a_re * dt)
    abar_r, abar_i = mag * jnp.cos(a_im * dt), mag * jnp.sin(a_im * dt)
    xr, xi = abar_r - 1.0, abar_i
    den = a_re * a_re + a_im * a_im
    coef_r = (xr * a_re + xi * a_im) / den
    coef_i = (xi * a_re - xr * a_im) / den
    bbar_r = coef_r[..., None] * b_re - coef_i[..., None] * b_im
    bbar_i = coef_r[..., None] * b_im + coef_i[..., None] * b_re
    eye = jnp.eye(gb, dtype=F32)

    def b_big(bb):
        blk = bb.reshape(nblk, gb, p, S5_GROUP_CH).transpose(0, 1, 3, 2)
        return blk[:, :, :, None, :] * eye[None, :, None, :, None]

    def c_big(cc):
        blk = cc.reshape(nblk, gb, S5_GROUP_CH, p).transpose(0, 1, 3, 2)
        return blk[:, :, :, None, :] * eye[None, :, None, :, None]

    n_state = gb * p
    return (abar_r.reshape(1, g * p), abar_i.reshape(1, g * p),
            b_big(bbar_r).reshape(nblk, S5_CH_BLOCK, n_state), b_big(bbar_i).reshape(nblk, S5_CH_BLOCK, n_state),
            c_big(c_re).reshape(nblk, n_state, S5_CH_BLOCK), c_big(c_im).reshape(nblk, n_state, S5_CH_BLOCK))


def _s5_core(h, disc, d_skip, s0r, s0i, nb, t, *, tt_cap=256):
    m, d = h.shape
    abar_r, abar_i, bbr, bbi, ccr, cci = disc
    nblk, cb, n_state = bbr.shape
    tt = _row_tile(t, tt_cap)
    nt = t // tt
    state = jax.ShapeDtypeStruct((nb, 1, nblk * n_state), F32)
    lane_spec = pl.BlockSpec((1, n_state), lambda b, c, i: (0, c))
    state_spec = pl.BlockSpec((None, 1, n_state), lambda b, c, i: (b, 0, c))
    b_spec = pl.BlockSpec((None, cb, n_state), lambda b, c, i: (c, 0, 0))
    c_spec = pl.BlockSpec((None, n_state, cb), lambda b, c, i: (c, 0, 0))
    row_spec = pl.BlockSpec((tt, cb), lambda b, c, i: (b * nt + i, c))
    vmem = 4 * tt * cb * 4 + 16 * cb * n_state * 4 + 6 * tt * n_state * 4
    return pl.pallas_call(
        functools.partial(_s5_body, tt=tt),
        grid=(nb, nblk, nt),
        in_specs=[row_spec, lane_spec, lane_spec, b_spec, b_spec, c_spec, c_spec,
                  pl.BlockSpec((1, cb), lambda b, c, i: (0, c)), state_spec, state_spec],
        out_specs=[row_spec, state_spec, state_spec],
        out_shape=[jax.ShapeDtypeStruct((m, d), F32), state, state],
        scratch_shapes=[pltpu.VMEM((tt, n_state), F32), pltpu.VMEM((tt, n_state), F32),
                        pltpu.VMEM((V7X_SUBLANES, n_state), F32), pltpu.VMEM((V7X_SUBLANES, n_state), F32),
                        pltpu.VMEM((1, n_state), F32), pltpu.VMEM((1, n_state), F32)],
        compiler_params=_params(vmem, 3),
        name="s5",
    )(h, abar_r, abar_i, bbr, bbi, ccr, cci, d_skip.reshape(1, d),
      s0r.reshape(nb, 1, -1), s0i.reshape(nb, 1, -1))


def _conv_body(b_ref, c_ref, v_ref, w_ref, p_ref, y_ref, s_ref):
    z = c_ref[...] * v_ref[...]
    t = z.shape[0]
    row = lax.broadcasted_iota(jnp.int32, z.shape, 0)
    p0, p1 = p_ref[0:1, :], p_ref[1:2, :]
    z1 = jnp.where(row >= 1, pltpu.roll(z, 1, 0), p1)
    z2 = jnp.where(row >= 2, pltpu.roll(z, 2, 0), jnp.where(row == 0, p0, p1))
    y = w_ref[0:1, :] * z2 + w_ref[1:2, :] * z1 + w_ref[2:3, :] * z
    y_ref[...] = b_ref[...] * y
    s_ref[...] = z[t - (CONV_W - 1):, :]


def _conv_core(bcv, w_conv, prev, nb, t, *, tc=512):
    m, d3 = bcv.shape
    d = d3 // 3
    nc = d // tc
    vmem = 12 * t * tc * 4
    return pl.pallas_call(
        _conv_body,
        grid=(nb, nc),
        in_specs=[
            pl.BlockSpec((t, tc), lambda b, j: (b, j)),
            pl.BlockSpec((t, tc), lambda b, j: (b, nc + j)),
            pl.BlockSpec((t, tc), lambda b, j: (b, 2 * nc + j)),
            pl.BlockSpec((CONV_W, tc), lambda b, j: (0, j)),
            pl.BlockSpec((None, CONV_W - 1, tc), lambda b, j: (b, 0, j)),
        ],
        out_specs=[pl.BlockSpec((t, tc), lambda b, j: (b, j)),
                   pl.BlockSpec((None, CONV_W - 1, tc), lambda b, j: (b, 0, j))],
        out_shape=[jax.ShapeDtypeStruct((m, d), F32), jax.ShapeDtypeStruct((nb, CONV_W - 1, d), F32)],
        compiler_params=_params(vmem, 2),
        name="conv",
    )(bcv, bcv, bcv, w_conv, prev)


def _gla_gate_body(x_ref, g_ref, w1_ref, w2_ref, b_ref, o_ref):
    h = _rms(x_ref[...], g_ref[...]).astype(BF16)
    low = _dot(h, w1_ref[...].astype(BF16))
    y = _dot(low.astype(BF16), w2_ref[...].astype(BF16)) + b_ref[...]
    o_ref[...] = -_softplus(-y) / GLA_TAU


def _gla_gate(x, g, w1, w2, b, *, tm_cap=512):
    m, d = x.shape
    r = w1.shape[1]
    gk = w2.shape[1]
    rp = V7X_LANES
    w1p = jnp.pad(w1, ((0, 0), (0, rp - r)))
    w2p = jnp.pad(w2, ((0, rp - r), (0, 0)))
    tm = _row_tile(m, tm_cap)
    vmem = 3 * tm * d * 4 + 4 * d * rp * 4 + 4 * rp * gk * 4 + 6 * tm * gk * 4
    return pl.pallas_call(
        _gla_gate_body,
        grid=(m // tm,),
        in_specs=[
            pl.BlockSpec((tm, d), lambda i: (i, 0)),
            pl.BlockSpec((1, d), lambda i: (0, 0)),
            pl.BlockSpec((d, rp), lambda i: (0, 0)),
            pl.BlockSpec((rp, gk), lambda i: (0, 0)),
            pl.BlockSpec((1, gk), lambda i: (0, 0)),
        ],
        out_specs=pl.BlockSpec((tm, gk), lambda i: (i, 0)),
        out_shape=jax.ShapeDtypeStruct((m, gk), F32),
        compiler_params=_params(vmem, 1),
        name="gla_gate",
    )(x, g.reshape(1, d), w1p, w2p, b.reshape(1, gk))


def _cumsum_rows(x):
    n = x.shape[0]
    row = lax.broadcasted_iota(jnp.int32, x.shape, 0)
    dist = 1
    while dist < n:
        x = x + jnp.where(row >= dist, pltpu.roll(x, dist, 0), 0.0)
        dist *= 2
    return x


def _gla_chunk(q, k, v, glog, st):
    c, dk = q.shape
    b = _cumsum_rows(glog)
    o = lax.dot_general((q * jnp.exp(b)).astype(BF16), st.astype(BF16), NT_DIMS, preferred_element_type=F32)
    rowc = lax.broadcasted_iota(jnp.int32, (c, 1), 0)
    lane = lax.broadcasted_iota(jnp.int32, (1, c), 1)
    rsub = lax.broadcasted_iota(jnp.int32, (GLA_SUB, 1), 0)
    att_rows = []
    for blk in range(c // GLA_SUB):
        lo = blk * GLA_SUB
        b_i = b[lo:lo + GLA_SUB]
        q_i = q[lo:lo + GLA_SUB]
        if blk > 0:
            b_0 = b[lo - 1:lo]
            q_off = (q_i * jnp.exp(b_i - b_0)).astype(BF16)
            k_off = (k * jnp.exp(jnp.where(rowc < lo, b_0 - b, NEG_INF))).astype(BF16)
            att = lax.dot_general(q_off, k_off, NT_DIMS, preferred_element_type=F32)
        else:
            att = jnp.zeros((GLA_SUB, c), F32)
        for jj in range(GLA_SUB):
            j = lo + jj
            dec = jnp.exp(jnp.where(rsub >= jj, b_i - b[j:j + 1], NEG_INF))
            col = jnp.sum(q_i * k[j:j + 1] * dec, axis=1, keepdims=True)
            att = att + jnp.where(lane == j, col, 0.0)
        att_rows.append(att)
    att = att_rows[0] if len(att_rows) == 1 else jnp.concatenate(att_rows, axis=0)
    vb = v.astype(BF16)
    o = o + _dot(att.astype(BF16), vb)
    b_last = b[c - 1:c]
    k_dec = (k * jnp.exp(b_last - b)).astype(BF16)
    st = st * jnp.exp(b_last) + lax.dot_general(vb, k_dec, TN_DIMS, preferred_element_type=F32)
    return o, st


def _gla_body(*refs, chunk, has_prev, scale):
    refs = list(refs)
    q_ref, k_ref, v_ref, r_ref, gl_ref, ng_ref = refs[:6]
    s0_ref = refs[6] if has_prev else None
    o_ref, s_ref, st_s = refs[-3:]
    t = pl.program_id(2)

    @pl.when(t == 0)
    def _():
        st_s[...] = s0_ref[...].T if has_prev else jnp.zeros_like(st_s)

    def step(ci, carry):
        rows = pl.ds(pl.multiple_of(ci * chunk, chunk), chunk)
        o, st = _gla_chunk(q_ref[rows, :] * scale, k_ref[rows, :], v_ref[rows, :], gl_ref[rows, :], st_s[...])
        st_s[...] = st
        o = o * lax.rsqrt(jnp.mean(o * o, axis=-1, keepdims=True) + EPS) * ng_ref[...]
        o_ref[rows, :] = o * jax.nn.silu(r_ref[rows, :])
        return carry

    lax.fori_loop(0, q_ref.shape[0] // chunk, step, 0)

    @pl.when(t == pl.num_programs(2) - 1)
    def _():
        s_ref[...] = st_s[...].T


def _gla_core(qkvr, glog, norm_g, prev, nb, t, *, tt_cap=256):
    m = qkvr.shape[0]
    gk = glog.shape[1]
    dk = gk // GLA_HEADS
    d = (qkvr.shape[1] - 2 * gk) // 2
    dv = d // GLA_HEADS
    chunk = GLA_CHUNK if t % GLA_CHUNK == 0 else t
    tt = _row_tile(t, tt_cap)
    nt = t // tt
    nqk = gk // dk
    in_specs = [
        pl.BlockSpec((tt, dk), lambda b, h, i: (b * nt + i, h)),
        pl.BlockSpec((tt, dk), lambda b, h, i: (b * nt + i, nqk + h)),
        pl.BlockSpec((tt, dv), lambda b, h, i: (b * nt + i, 2 * gk // dv + h)),
        pl.BlockSpec((tt, dv), lambda b, h, i: (b * nt + i, (2 * gk + d) // dv + h)),
        pl.BlockSpec((tt, dk), lambda b, h, i: (b * nt + i, h)),
        pl.BlockSpec((1, dv), lambda b, h, i: (0, 0)),
    ]
    args = [qkvr, qkvr, qkvr, qkvr, glog, norm_g.reshape(1, dv)]
    state_spec = pl.BlockSpec((None, None, dk, dv), lambda b, h, i: (b, h, 0, 0))
    if prev is not None:
        in_specs.append(state_spec)
        args.append(prev)
    vmem = 2 * tt * (3 * dk + 3 * dv) * 4 + 7 * dk * dv * 4 + 64 * chunk * dv * 4
    return pl.pallas_call(
        functools.partial(_gla_body, chunk=chunk, has_prev=prev is not None, scale=dk ** -0.5),
        grid=(nb, GLA_HEADS, nt),
        in_specs=in_specs,
        out_specs=[pl.BlockSpec((tt, dv), lambda b, h, i: (b * nt + i, h)), state_spec],
        out_shape=[jax.ShapeDtypeStruct((m, d), F32), jax.ShapeDtypeStruct((nb, GLA_HEADS, dk, dv), F32)],
        scratch_shapes=[pltpu.VMEM((dv, dk), F32)],
        compiler_params=_params(vmem, 3),
        name="gla",
    )(*args)


def _trunk(x, nb, t, mem_k, mem_v, p, s5_disc, sample):
    d = x.shape[1]
    depth = p["norm_mix"].shape[0]
    new = {}
    for i in range(depth):
        g_mix = p["norm_mix"][i]
        kind = i % 4
        if kind == 0:
            qkv = _mm(x, p["w_sb_qkv"], g=g_mix)
            if sample is None:
                o = _sb_attend_prompt(qkv, p["sb_bias"], nb, t)
            else:
                o = _sb_attend_sample(qkv, p["sb_bias"], sample["cache_sb_k"], sample["cache_sb_v"],
                                      sample["page_table"], nb, t)
            x = _mm(o, p["w_sb_out"], res=x)
            shp = (nb, t, SB_HEADS, d // SB_HEADS)
            new["sb"] = (qkv[:, d:2 * d].reshape(shp), qkv[:, 2 * d:].reshape(shp))
        elif kind == 1:
            h = _rmsnorm(x, g_mix)
            n_state = d // S5_GROUP_CH * S5_STATE
            if sample is None:
                s0r = s0i = jnp.zeros((nb, n_state), F32)
            else:
                s0r, s0i = sample["state_s5_re"], sample["state_s5_im"]
            y, sr, si = _s5_core(h, s5_disc, p["s5_d"], s0r, s0i, nb, t)
            x = _mm(y, p["w_s5_glu"], res=x, glu=True)
            sshape = (nb, d // S5_GROUP_CH, S5_STATE)
            new["s5"] = (sr.reshape(sshape), si.reshape(sshape))
        elif kind == 2:
            bcv = _mm(x, p["w_conv_in"], g=g_mix)
            prev = jnp.zeros((nb, CONV_W - 1, d), F32) if sample is None else sample["state_conv"]
            yg, new["conv"] = _conv_core(bcv, p["w_conv"], prev, nb, t)
            x = _mm(yg, p["w_conv_out"], res=x)
        else:
            qkvr = _mm(x, p["w_gla_in"], g=g_mix)
            glog = _gla_gate(x, g_mix, p["w_gla_g1"], p["w_gla_g2"], p["b_gla_g"])
            o, new["gla"] = _gla_core(qkvr, glog, p["gla_norm"], None if sample is None else sample["state_gla"], nb, t)
            x = _mm(o, p["w_gla_out"], res=x)
        x = _xattn(x, p["norm_xattn"][i], p["w_xq"], p["w_xo"], mem_k, mem_v, i, t)
        x = _ffn(x, p["norm_ffn"][i], p["w_ffn_in"], p["w_ffn_out"], i)
    y = _rmsnorm(x, p["norm_final"])
    return y.reshape(nb, t, d), new


def kernel(x_prompt, x_sample, mem_prompt, cache_sb_k, cache_sb_v, page_table, state_s5_re, state_s5_im, state_conv, state_gla, cache_mem_k, cache_mem_v, norm_mix, norm_xattn, norm_mem, norm_ffn, norm_final, w_sb_qkv, w_sb_out, sb_bias, s5_a_re, s5_a_im, s5_log_dt, s5_b_re, s5_b_im, s5_c_re, s5_c_im, s5_d, w_s5_glu, w_conv_in, w_conv, w_conv_out, w_gla_in, w_gla_g1, w_gla_g2, b_gla_g, gla_norm, w_gla_out, w_xq, w_xk, w_xv, w_xo, w_ffn_in, w_ffn_out):
    p = dict(norm_mix=norm_mix, norm_xattn=norm_xattn, norm_ffn=norm_ffn, norm_final=norm_final,
             w_sb_qkv=w_sb_qkv, w_sb_out=w_sb_out, sb_bias=sb_bias, s5_d=s5_d, w_s5_glu=w_s5_glu,
             w_conv_in=w_conv_in, w_conv=w_conv, w_conv_out=w_conv_out,
             w_gla_in=w_gla_in, w_gla_g1=w_gla_g1, w_gla_g2=w_gla_g2, b_gla_g=b_gla_g,
             gla_norm=gla_norm, w_gla_out=w_gla_out, w_xq=w_xq, w_xo=w_xo,
             w_ffn_in=w_ffn_in, w_ffn_out=w_ffn_out)
    nbp, tp, d = x_prompt.shape
    nbs, ts, _ = x_sample.shape
    depth = norm_mix.shape[0]
    n_mem = mem_prompt.shape[1]
    xw = w_xk.shape[-1]

    mem_k_p, mem_v_p = _memkv(mem_prompt.reshape(nbp * n_mem, d), norm_mem, w_xk, w_xv)
    mem_k_p = mem_k_p.reshape(depth, nbp, n_mem, xw)
    mem_v_p = mem_v_p.reshape(depth, nbp, n_mem, xw)
    s5_disc = _s5_discretize(s5_a_re, s5_a_im, s5_log_dt, s5_b_re, s5_b_im, s5_c_re, s5_c_im)

    y_p, new_p = _trunk(x_prompt.reshape(nbp * tp, d), nbp, tp, mem_k_p, mem_v_p, p, s5_disc, None)
    sample = dict(cache_sb_k=cache_sb_k, cache_sb_v=cache_sb_v, page_table=page_table,
                  state_s5_re=state_s5_re.reshape(nbs, -1), state_s5_im=state_s5_im.reshape(nbs, -1),
                  state_conv=state_conv, state_gla=state_gla)
    y_s, new_s = _trunk(x_sample.reshape(nbs * ts, d), nbs, ts,
                        cache_mem_k.reshape(depth, nbs, n_mem, xw), cache_mem_v.reshape(depth, nbs, n_mem, xw),
                        p, s5_disc, sample)

    mshape = (depth, nbp, n_mem, X_HEADS, X_HEAD_DIM)
    return (y_p, y_s, new_p["sb"][0], new_p["sb"][1], new_s["sb"][0], new_s["sb"][1],
            new_p["s5"][0], new_p["s5"][1], new_s["s5"][0], new_s["s5"][1],
            new_p["conv"], new_s["conv"], new_p["gla"], new_s["gla"],
            mem_k_p.reshape(mshape), mem_v_p.reshape(mshape))
```

```python
import functools
import math

import jax
import jax.numpy as jnp
from jax import lax
from jax.experimental import pallas as pl
from jax.experimental.pallas import tpu as pltpu

F32 = jnp.float32
BF16 = jnp.bfloat16
EPS = 1e-6
NEG_INF = float("-inf")
LOG2E = 1.4426950408889634

V7X_VMEM_BYTES = 64 * 1024 * 1024
V7X_SUBLANES = 8
V7X_LANES = 128

SB_HEADS = 16
SB_KEY_TILE = 256
S5_GROUP_CH = 16
S5_STATE = 64
S5_CH_BLOCK = 256
CONV_W = 3
GLA_HEADS = 4
GLA_TAU = 16.0
GLA_CHUNK = 32
GLA_SUB = 8
X_HEADS = 4
X_HEAD_DIM = 128

NT_DIMS = (((1,), (1,)), ((), ()))
TN_DIMS = (((0,), (0,)), ((), ()))


def _params(vmem_bytes, n_grid):
    limit = int(min(V7X_VMEM_BYTES - (6 << 20), max(vmem_bytes * 5 // 4 + (4 << 20), 16 << 20)))
    return pltpu.CompilerParams(dimension_semantics=("arbitrary",) * n_grid, vmem_limit_bytes=limit)


def _rms(x, g):
    return x * lax.rsqrt(jnp.mean(x * x, axis=-1, keepdims=True) + EPS) * g


def _softplus(z):
    return jnp.maximum(z, 0.0) + jnp.log1p(jnp.exp(-jnp.abs(z)))


def _dot(a, b):
    return jnp.dot(a, b, preferred_element_type=F32)


def _split_hi_lo(x):
    hi = x.astype(BF16)
    lo = (x - hi.astype(F32)).astype(BF16)
    return hi, lo


def _row_tile(m, cap):
    t = min(m, cap)
    while m % t:
        t //= 2
    return t


def _mm_body(*refs, norm, mode, dual):
    refs = list(refs)
    x_ref = refs.pop(0)
    g_ref = refs.pop(0) if norm else None
    w_ref = refs.pop(0)
    w2_ref = refs.pop(0) if mode == "glu" else None
    has_res = mode in ("res", "glu")
    res_ref = refs.pop(0) if has_res else None
    xs_ref = refs.pop(0) if dual else None
    ress_ref = refs.pop(0) if dual and has_res else None
    o_ref = refs.pop(0)
    os_ref = refs.pop(0) if dual else None
    lhs = refs.pop(0)
    lhs_s = refs.pop(0) if dual else None
    first_col = pl.program_id(1) == 0

    def load_lhs(src, dst):
        x = src[...]
        if norm:
            x = _rms(x, g_ref[...])
        dst[...] = x.astype(BF16)

    def emit(a, res, out):
        y = _dot(a, w_ref[...].astype(BF16))
        if mode == "glu":
            y = y * jax.nn.sigmoid(_dot(a, w2_ref[...].astype(BF16)))
        if res is not None:
            y = res[...] + y
        out[...] = y

    pl.when(first_col)(lambda: load_lhs(x_ref, lhs))
    emit(lhs[...], res_ref, o_ref)
    if dual:
        @pl.when(pl.program_id(0) == 0)
        def _():
            pl.when(first_col)(lambda: load_lhs(xs_ref, lhs_s))
            emit(lhs_s[...], ress_ref, os_ref)


def _mm(x, w, *, g=None, res=None, glu=False, lhs_time_major=None, small=None, small_res=None,
        tn=512, tm_cap=1024):
    m, k = x.shape
    n = w.shape[1] // (2 if glu else 1)
    nj = n // tn
    mode = "glu" if glu else ("res" if res is not None else "plain")
    dual = small is not None
    if lhs_time_major is None:
        tm = _row_tile(m, tm_cap)
        in_specs = [pl.BlockSpec((tm, k), lambda i, j: (i, 0))]
    else:
        nb, t = lhs_time_major
        tm = _row_tile(t, tm_cap)
        tiles = t // tm
        x = x.reshape(t, nb * k)
        in_specs = [pl.BlockSpec((tm, k), lambda i, j: (i % tiles, i // tiles))]
    args = [x]
    if g is not None:
        in_specs.append(pl.BlockSpec((1, k), lambda i, j: (0, 0)))
        args.append(g.reshape(1, k))
    in_specs.append(pl.BlockSpec((k, tn), lambda i, j: (0, j)))
    args.append(w)
    if glu:
        in_specs.append(pl.BlockSpec((k, tn), lambda i, j: (0, j + nj)))
        args.append(w)
    if res is not None:
        in_specs.append(pl.BlockSpec((tm, tn), lambda i, j: (i, j)))
        args.append(res)
    out_specs = [pl.BlockSpec((tm, tn), lambda i, j: (i, j))]
    out_shape = [jax.ShapeDtypeStruct((m, n), F32)]
    scratch = [pltpu.VMEM((tm, k), BF16)]
    ms = 0
    if dual:
        ms = small.shape[0]
        small_cols = pl.BlockSpec((ms, tn), lambda i, j: (0, jnp.where(i == 0, j, nj - 1)))
        in_specs.append(pl.BlockSpec((ms, k), lambda i, j: (0, 0)))
        args.append(small)
        if res is not None:
            in_specs.append(small_cols)
            args.append(small_res)
        out_specs.append(small_cols)
        out_shape.append(jax.ShapeDtypeStruct((ms, n), F32))
        scratch.append(pltpu.VMEM((ms, k), BF16))
    vmem = (2 * (tm + ms) * k * 4 + (tm + ms) * k * 2 + (2 if glu else 1) * 2 * k * tn * 4
            + 4 * (tm + ms) * tn * 4 + 2 * k * tn * 4)
    outs = pl.pallas_call(
        functools.partial(_mm_body, norm=g is not None, mode=mode, dual=dual),
        grid=(m // tm, nj),
        in_specs=in_specs,
        out_specs=out_specs,
        out_shape=out_shape,
        scratch_shapes=scratch,
        compiler_params=_params(vmem, 2),
        name="mm_" + mode,
    )(*args)
    return tuple(outs) if dual else outs[0]


def _ffn_body(*refs, dual):
    if dual:
        x_ref, g_ref, wg_ref, wu_ref, wo_ref, xs_ref, o_ref, os_ref, h_scr, hs_scr = refs
    else:
        x_ref, g_ref, wg_ref, wu_ref, wo_ref, o_ref, h_scr = refs
    first_col = pl.program_id(1) == 0

    def start(src, h_dst, out):
        x = src[...]
        h_dst[...] = _rms(x, g_ref[...]).astype(BF16)
        out[...] = x

    def accumulate(h, out):
        gate = _dot(h, wg_ref[...].astype(BF16))
        up = _dot(h, wu_ref[...].astype(BF16))
        a = (jax.nn.silu(gate) * up).astype(BF16)
        out[...] += _dot(a, wo_ref[...].astype(BF16))

    pl.when(first_col)(lambda: start(x_ref, h_scr, o_ref))
    accumulate(h_scr[...], o_ref)
    if dual:
        @pl.when(pl.program_id(0) == 0)
        def _():
            pl.when(first_col)(lambda: start(xs_ref, hs_scr, os_ref))
            accumulate(hs_scr[...], os_ref)


def _ffn(x, g, w_in, w_out, layer, *, small=None, tf=256, tm_cap=1024):
    m, d = x.shape
    f = w_out.shape[1]
    tm = _row_tile(m, tm_cap)
    nf = f // tf
    dual = small is not None
    ms = small.shape[0] if dual else 0
    in_specs = [
        pl.BlockSpec((tm, d), lambda i, j: (i, 0)),
        pl.BlockSpec((1, d), lambda i, j: (0, 0)),
        pl.BlockSpec((None, d, tf), lambda i, j: (layer, 0, j)),
        pl.BlockSpec((None, d, tf), lambda i, j: (layer, 0, j + nf)),
        pl.BlockSpec((None, tf, d), lambda i, j: (layer, j, 0)),
    ]
    args = [x, g.reshape(1, d), w_in, w_in, w_out]
    out_specs = [pl.BlockSpec((tm, d), lambda i, j: (i, 0))]
    out_shape = [jax.ShapeDtypeStruct((m, d), F32)]
    scratch = [pltpu.VMEM((tm, d), BF16)]
    if dual:
        in_specs.append(pl.BlockSpec((ms, d), lambda i, j: (0, 0)))
        args.append(small)
        out_specs.append(pl.BlockSpec((ms, d), lambda i, j: (0, 0)))
        out_shape.append(jax.ShapeDtypeStruct((ms, d), F32))
        scratch.append(pltpu.VMEM((ms, d), BF16))
    vmem = (4 * (tm + ms) * d * 4 + (tm + ms) * d * 2 + 3 * 2 * d * tf * 4 + 3 * d * tf * 2
            + 3 * (tm + ms) * tf * 4)
    outs = pl.pallas_call(
        functools.partial(_ffn_body, dual=dual),
        grid=(m // tm, nf),
        in_specs=in_specs,
        out_specs=out_specs,
        out_shape=out_shape,
        scratch_shapes=scratch,
        compiler_params=_params(vmem, 2),
        name="ffn",
    )(*args)
    return tuple(outs) if dual else outs[0]


def _xattn_body(x_ref, g_ref, wq_ref, wo_ref, mk_ref, mv_ref, o_ref, wq_s, wo_s):
    @pl.when(pl.program_id(0) == 0)
    def _():
        wq_s[...] = wq_ref[...].astype(BF16)
        wo_s[...] = wo_ref[...].astype(BF16)

    x = x_ref[...]
    h = _rms(x, g_ref[...]).astype(BF16)
    q = _dot(h, wq_s[...])
    mk = mk_ref[...].astype(BF16)
    mv = mv_ref[...].astype(BF16)
    heads = []
    for hh in range(X_HEADS):
        sl = slice(hh * X_HEAD_DIM, (hh + 1) * X_HEAD_DIM)
        s = lax.dot_general(q[:, sl].astype(BF16), mk[:, sl], NT_DIMS, preferred_element_type=F32)
        s = s * X_HEAD_DIM ** -0.5
        e = jnp.exp(s - jnp.max(s, axis=-1, keepdims=True))
        a = e / jnp.sum(e, axis=-1, keepdims=True)
        heads.append(_dot(a.astype(BF16), mv[:, sl]))
    o = jnp.concatenate(heads, axis=1).astype(BF16)
    o_ref[...] = x + _dot(o, wo_s[...])


def _xattn(x, g, w_q, w_o, mem_k, mem_v, layer, rows_per_batch, *, tm_cap=512):
    m, d = x.shape
    xw = w_q.shape[-1]
    n_mem = mem_k.shape[2]
    tm = _row_tile(rows_per_batch, tm_cap)
    tiles_per_batch = rows_per_batch // tm
    vmem = 4 * tm * d * 4 + 2 * 2 * d * xw * 4 + 2 * d * xw * 2 + 8 * n_mem * xw * 4 + 6 * tm * xw * 4
    mem_spec = pl.BlockSpec((None, None, n_mem, xw), lambda i: (layer, i // tiles_per_batch, 0, 0))
    return pl.pallas_call(
        _xattn_body,
        grid=(m // tm,),
        in_specs=[
            pl.BlockSpec((tm, d), lambda i: (i, 0)),
            pl.BlockSpec((1, d), lambda i: (0, 0)),
            pl.BlockSpec((None, d, xw), lambda i: (layer, 0, 0)),
            pl.BlockSpec((None, xw, d), lambda i: (layer, 0, 0)),
            mem_spec,
            mem_spec,
        ],
        out_specs=pl.BlockSpec((tm, d), lambda i: (i, 0)),
        out_shape=jax.ShapeDtypeStruct((m, d), F32),
        scratch_shapes=[pltpu.VMEM((d, xw), BF16), pltpu.VMEM((xw, d), BF16)],
        compiler_params=_params(vmem, 1),
        name="xattn",
    )(x, g.reshape(1, d), w_q, w_o, mem_k, mem_v)


def _memkv_body(m_ref, g_ref, wk_ref, wv_ref, k_ref, v_ref):
    mn = _rms(m_ref[...], g_ref[...]).astype(BF16)
    k_ref[...] = _dot(mn, wk_ref[...].astype(BF16))
    v_ref[...] = _dot(mn, wv_ref[...].astype(BF16))


def _memkv(mem, norm_mem, w_xk, w_xv):
    depth, d, xw = w_xk.shape
    m = mem.shape[0]
    out = jax.ShapeDtypeStruct((depth, m, xw), F32)
    vmem = 2 * m * d * 4 + m * d * 2 + 4 * d * xw * 4 + 2 * d * xw * 2 + 6 * m * xw * 4
    return pl.pallas_call(
        _memkv_body,
        grid=(depth,),
        in_specs=[
            pl.BlockSpec((m, d), lambda l: (0, 0)),
            pl.BlockSpec((None, 1, d), lambda l: (l, 0, 0)),
            pl.BlockSpec((None, d, xw), lambda l: (l, 0, 0)),
            pl.BlockSpec((None, d, xw), lambda l: (l, 0, 0)),
        ],
        out_specs=[pl.BlockSpec((None, m, xw), lambda l: (l, 0, 0))] * 2,
        out_shape=[out, out],
        compiler_params=_params(vmem, 1),
        name="memkv",
    )(mem, norm_mem.reshape(depth, 1, d), w_xk, w_xv)


def _rmsnorm_body(x_ref, g_ref, o_ref):
    o_ref[...] = _rms(x_ref[...], g_ref[...])


def _rmsnorm(x, g, *, time_major=None, tm_cap=512):
    m, d = x.shape
    if time_major is None:
        tm = _row_tile(m, tm_cap)
        grid = (1, m // tm)
        out_spec = pl.BlockSpec((tm, d), lambda b, i: (i, 0))
        out_shape = jax.ShapeDtypeStruct((m, d), F32)
    else:
        nb, t = time_major
        tm = _row_tile(t, tm_cap)
        grid = (nb, t // tm)
        out_spec = pl.BlockSpec((tm, d), lambda b, i: (i, b))
        out_shape = jax.ShapeDtypeStruct((t, nb * d), F32)
    nt = grid[1]
    y = pl.pallas_call(
        _rmsnorm_body,
        grid=grid,
        in_specs=[pl.BlockSpec((tm, d), lambda b, i: (b * nt + i, 0)), pl.BlockSpec((1, d), lambda b, i: (0, 0))],
        out_specs=out_spec,
        out_shape=out_shape,
        compiler_params=_params(6 * tm * d * 4, 2),
        name="rmsnorm",
    )(x, g.reshape(1, d))
    return y.reshape(m, d)


def _suffix_matrix(n):
    r = lax.broadcasted_iota(jnp.int32, (n, n), 0)
    c = lax.broadcasted_iota(jnp.int32, (n, n), 1)
    return jnp.where(r >= c, 1.0, 0.0).astype(BF16)


def _sb_suffix(z, mask, u_mat):
    nz = -z
    lk = jnp.minimum(nz, 0.0) - jnp.log2(1.0 + jnp.exp2(jnp.minimum(z, nz)))
    if mask is not None:
        lk = jnp.where(mask, lk, 0.0)
    hi, lo = _split_hi_lo(lk)
    return _dot(hi, u_mat) + _dot(lo, u_mat)


def _sb_weights(z, incl, mask, acc):
    w = jnp.exp2(z + incl + acc)
    return w if mask is None else jnp.where(mask, w, 0.0)


def _sb_tile(z, mask, acc, u_mat):
    incl = _sb_suffix(z, mask, u_mat)
    return _sb_weights(z, incl, mask, acc), acc + incl[:, 0:1]


def _sbp_body(bias_ref, q_ref, k_ref, v_ref, o_ref, *, tq, tk, scale):
    h = pl.program_id(1)
    i = pl.program_id(2)
    bias = bias_ref[h] * LOG2E
    q = (q_ref[...] * (scale * LOG2E)).astype(BF16)
    u_mat = _suffix_matrix(tk)
    n_diag = tq // tk
    nk = (i + 1) * n_diag

    def group(m, carry, masked):
        acc, out = carry
        starts = [pl.multiple_of((nk - 1 - (n_diag * m + r)) * tk, tk) for r in range(n_diag)]
        zs = [lax.dot_general(q, k_ref[pl.ds(s, tk), :].astype(BF16), NT_DIMS, preferred_element_type=F32) + bias
              for s in starts]
        masks = [None] * n_diag
        if masked:
            qoff = lax.broadcasted_iota(jnp.int32, (tq, tk), 0)
            koff = lax.broadcasted_iota(jnp.int32, (tq, tk), 1)
            masks = [(koff - qoff) < (i * tq - s) for s in starts]
        incls = [_sb_suffix(z, mask, u_mat) for z, mask in zip(zs, masks)]
        for s, z, incl, mask in zip(starts, zs, incls, masks):
            w = _sb_weights(z, incl, mask, acc).astype(BF16)
            acc = acc + incl[:, 0:1]
            out = out + _dot(w, v_ref[pl.ds(s, tk), :].astype(BF16))
        return acc, out

    dh = q_ref.shape[1]
    carry = group(0, (jnp.zeros((tq, 1), F32), jnp.zeros((tq, dh), F32)), True)
    _, out = lax.fori_loop(1, i + 1, lambda m, c: group(m, c, False), carry)
    o_ref[...] = out


def _sb_attend_prompt(qkv, bias, nb, t, *, tq=512):
    m, d3 = qkv.shape
    d = d3 // 3
    dh = d // SB_HEADS
    tq = min(tq, t)
    tk = min(SB_KEY_TILE, tq)
    nq = t // tq
    vmem = 4 * t * dh * 4 + 4 * tq * dh * 4 + 16 * tq * tk * 4
    return pl.pallas_call(
        functools.partial(_sbp_body, tq=tq, tk=tk, scale=dh ** -0.5),
        grid=(nb, SB_HEADS, nq),
        in_specs=[
            pl.BlockSpec(memory_space=pltpu.SMEM),
            pl.BlockSpec((tq, dh), lambda b, h, i: (b * nq + i, h)),
            pl.BlockSpec((t, dh), lambda b, h, i: (b, SB_HEADS + h)),
            pl.BlockSpec((t, dh), lambda b, h, i: (b, 2 * SB_HEADS + h)),
        ],
        out_specs=pl.BlockSpec((tq, dh), lambda b, h, i: (b * nq + i, h)),
        out_shape=jax.ShapeDtypeStruct((m, d), F32),
        compiler_params=_params(vmem, 3),
        name="sb_prompt",
    )(bias, qkv, qkv, qkv)


def _sbs_body(pt_ref, bias_ref, qkv_ref, *refs, t, dh, page, npp, scale):
    del pt_ref
    k_refs, v_refs = refs[:npp], refs[npp:2 * npp]
    o_ref, out_s, acc_s = refs[2 * npp:]
    s = pl.program_id(1)
    d = SB_HEADS * dh
    rows = SB_HEADS * t
    u_mat = _suffix_matrix(page)

    def attend(pages, mask, acc, outs):
        qs = [(qkv_ref[:, h * dh:(h + 1) * dh] * (scale * LOG2E)).astype(BF16) for h in range(SB_HEADS)]
        zs = [jnp.concatenate([lax.dot_general(qs[h], get_k(h), NT_DIMS, preferred_element_type=F32)
                               + bias_ref[h] * LOG2E for h in range(SB_HEADS)], axis=0)
              for get_k, _ in pages]
        incls = [_sb_suffix(z, mask, u_mat) for z in zs]
        ws = []
        for z, incl in zip(zs, incls):
            ws.append(_sb_weights(z, incl, mask, acc))
            acc = acc + incl[:, 0:1]
        for w, (_, get_v) in zip(ws, pages):
            outs = [outs[h] + _dot(w[h * t:(h + 1) * t, :].astype(BF16), get_v(h)) for h in range(SB_HEADS)]
        return acc, outs

    def store(acc, outs):
        acc_s[...] = acc
        for h in range(SB_HEADS):
            out_s[h * t:(h + 1) * t, :] = outs[h]

    @pl.when(s == 0)
    def _():
        pad = jnp.zeros((page - t, dh), F32)
        key = lax.broadcasted_iota(jnp.int32, (rows, page), 1)
        qry = lax.broadcasted_iota(jnp.int32, (rows, page), 0) % t
        new_tokens = (
            lambda h: jnp.concatenate([qkv_ref[:, d + h * dh:d + (h + 1) * dh], pad], axis=0).astype(BF16),
            lambda h: jnp.concatenate([qkv_ref[:, 2 * d + h * dh:2 * d + (h + 1) * dh], pad], axis=0).astype(BF16))
        acc, outs = attend([new_tokens], key < qry, jnp.zeros((rows, 1), F32),
                           [jnp.zeros((t, dh), F32)] * SB_HEADS)
        store(acc, outs)

    def cached_page(j):
        return (lambda h: k_refs[j][:, h * dh:(h + 1) * dh].astype(BF16),
                lambda h: v_refs[j][:, h * dh:(h + 1) * dh].astype(BF16))

    acc, outs = attend([cached_page(j) for j in range(npp)], None, acc_s[...],
                       [out_s[h * t:(h + 1) * t, :] for h in range(SB_HEADS)])
    store(acc, outs)

    @pl.when(s == pl.num_programs(1) - 1)
    def _():
        for h in range(SB_HEADS):
            o_ref[:, h * dh:(h + 1) * dh] = outs[h]


def _sb_attend_sample(qkv, bias, cache_k, cache_v, page_table, nb, t, *, npp=4):
    m, d3 = qkv.shape
    d = d3 // 3
    dh = d // SB_HEADS
    n_phys, page = cache_k.shape[0], cache_k.shape[1]
    n_pages = page_table.shape[1]
    while n_pages % npp:
        npp //= 2
    ck = cache_k.reshape(n_phys, page, d)
    cv = cache_v.reshape(n_phys, page, d)

    def cache_spec(j):
        return pl.BlockSpec((None, page, d), lambda b, s, pt: (pt[b, n_pages - 1 - (s * npp + j)], 0, 0))

    vmem = 4 * npp * page * d * 4 + 2 * t * d3 * 4 + 24 * npp * SB_HEADS * t * page * 4
    return pl.pallas_call(
        functools.partial(_sbs_body, t=t, dh=dh, page=page, npp=npp, scale=dh ** -0.5),
        grid_spec=pltpu.PrefetchScalarGridSpec(
            num_scalar_prefetch=1,
            grid=(nb, n_pages // npp),
            in_specs=[pl.BlockSpec(memory_space=pltpu.SMEM), pl.BlockSpec((t, d3), lambda b, s, pt: (b, 0))]
            + [cache_spec(j) for j in range(npp)] * 2,
            out_specs=pl.BlockSpec((t, d), lambda b, s, pt: (b, 0)),
            scratch_shapes=[pltpu.VMEM((SB_HEADS * t, dh), F32), pltpu.VMEM((SB_HEADS * t, 1), F32)],
        ),
        out_shape=jax.ShapeDtypeStruct((m, d), F32),
        compiler_params=_params(vmem, 2),
        name="sb_sample",
    )(page_table, bias, qkv, *([ck] * npp), *([cv] * npp))


def _cmul(ar, ai, br, bi):
    return ar * br - ai * bi, ar * bi + ai * br


def _gelu_tanh(x):
    return 0.5 * x * (1.0 + jnp.tanh(math.sqrt(2.0 / math.pi) * (x + 0.044715 * (x * x * x))))


def _s5_body(h_ref, ar_ref, ai_ref, br_ref, bi_ref, cr_ref, ci_ref, d_ref, s0r_ref, s0i_ref,
             y_ref, sr_ref, si_ref, xr_s, xi_s, pr_s, pi_s, cr_s, ci_s, *, nbk):
    t = pl.program_id(2)
    rows8, n_state = pr_s.shape
    row = lax.broadcasted_iota(jnp.int32, (rows8, n_state), 0)

    def last_step_to_all_rows(x):
        x = jnp.where(row >= rows8 - nbk, x, 0.0)
        have = nbk
        while have < rows8:
            x = x + pltpu.roll(x, rows8 - have, 0)
            have *= 2
        return x

    @pl.when(t == 0)
    def _():
        ar, ai = ar_ref[...], ai_ref[...]
        pr, pi = ar, ai
        for r in range(rows8):
            if r and r % nbk == 0:
                pr, pi = _cmul(pr, pi, ar, ai)
            pr_s[r:r + 1, :] = pr
            pi_s[r:r + 1, :] = pi
        cr_s[...] = jnp.zeros_like(cr_s)
        ci_s[...] = jnp.zeros_like(ci_s)
        cr_s[rows8 - nbk:, :] = s0r_ref[...]
        ci_s[rows8 - nbk:, :] = s0i_ref[...]
        cr_s[...] = last_step_to_all_rows(cr_s[...])
        ci_s[...] = last_step_to_all_rows(ci_s[...])

    h = h_ref[...]
    u = h.astype(BF16)
    xr_s[...] = _dot(u, br_ref[...].astype(BF16))
    xi_s[...] = _dot(u, bi_ref[...].astype(BF16))
    pw_r, pw_i = pr_s[...], pi_s[...]

    def tile(blk, carry):
        c_r, c_i = carry
        rows = pl.ds(pl.multiple_of(blk * rows8, rows8), rows8)
        x_r, x_i = xr_s[rows, :], xi_s[rows, :]
        dist = 1
        while dist * nbk < rows8:
            lo = (dist - 1) * nbk
            a_r, a_i = pw_r[lo:lo + 1, :], pw_i[lo:lo + 1, :]
            keep = row >= dist * nbk
            sh_r = jnp.where(keep, pltpu.roll(x_r, dist * nbk, 0), 0.0)
            sh_i = jnp.where(keep, pltpu.roll(x_i, dist * nbk, 0), 0.0)
            d_r, d_i = _cmul(a_r, a_i, sh_r, sh_i)
            x_r, x_i = x_r + d_r, x_i + d_i
            dist *= 2
        d_r, d_i = _cmul(pw_r, pw_i, c_r, c_i)
        x_r, x_i = x_r + d_r, x_i + d_i
        xr_s[rows, :] = x_r
        xi_s[rows, :] = x_i
        return last_step_to_all_rows(x_r), last_step_to_all_rows(x_i)

    c_r, c_i = lax.fori_loop(0, xr_s.shape[0] // rows8, tile, (cr_s[...], ci_s[...]))
    cr_s[...] = c_r
    ci_s[...] = c_i

    y = _dot(xr_s[...].astype(BF16), cr_ref[...].astype(BF16)) - _dot(xi_s[...].astype(BF16), ci_ref[...].astype(BF16))
    y_ref[...] = _gelu_tanh(y + d_ref[...] * h)

    @pl.when(t == pl.num_programs(2) - 1)
    def _():
        sr_ref[...] = c_r[rows8 - nbk:, :]
        si_ref[...] = c_i[rows8 - nbk:, :]


def _s5_discretize(a_re, a_im, log_dt, b_re, b_im, c_re, c_im):
    g, p = a_re.shape
    gb = S5_CH_BLOCK // S5_GROUP_CH
    nblk = g // gb
    dt = jnp.exp(log_dt)[:, None]
    mag = jnp.exp(a_re * dt)
    abar_r, abar_i = mag * jnp.cos(a_im * dt), mag * jnp.sin(a_im * dt)
    xr, xi = abar_r - 1.0, abar_i
    den = a_re * a_re + a_im * a_im
    coef_r = (xr * a_re + xi * a_im) / den
    coef_i = (xi * a_re - xr * a_im) / den
    bbar_r = coef_r[..., None] * b_re - coef_i[..., None] * b_im
    bbar_i = coef_r[..., None] * b_im + coef_i[..., None] * b_re
    eye = jnp.eye(gb, dtype=F32)

    def b_big(bb):
        blk = bb.reshape(nblk, gb, p, S5_GROUP_CH).transpose(0, 1, 3, 2)
        return blk[:, :, :, None, :] * eye[None, :, None, :, None]

    def c_big(cc):
        blk = cc.reshape(nblk, gb, S5_GROUP_CH, p).transpose(0, 1, 3, 2)
        return blk[:, :, :, None, :] * eye[None, :, None, :, None]

    n_state = gb * p
    return (abar_r.reshape(1, g * p), abar_i.reshape(1, g * p),
            b_big(bbar_r).reshape(nblk, S5_CH_BLOCK, n_state), b_big(bbar_i).reshape(nblk, S5_CH_BLOCK, n_state),
            c_big(c_re).reshape(nblk, n_state, S5_CH_BLOCK), c_big(c_im).reshape(nblk, n_state, S5_CH_BLOCK))


def _s5_core(h, disc, d_skip, s0r, s0i, nb, t, nbk, *, tt_cap=256):
    m, d = h.shape
    assert nbk in (1, nb) and V7X_SUBLANES % nbk == 0
    abar_r, abar_i, bbr, bbi, ccr, cci = disc
    nblk, cb, n_state = bbr.shape
    tt = _row_tile(t, tt_cap)
    nt = t // tt
    ng = nb // nbk
    state = jax.ShapeDtypeStruct((ng, nbk, nblk * n_state), F32)
    lane_spec = pl.BlockSpec((1, n_state), lambda b, c, i: (0, c))
    state_spec = pl.BlockSpec((None, nbk, n_state), lambda b, c, i: (b, 0, c))
    b_spec = pl.BlockSpec((None, cb, n_state), lambda b, c, i: (c, 0, 0))
    c_spec = pl.BlockSpec((None, n_state, cb), lambda b, c, i: (c, 0, 0))
    row_spec = pl.BlockSpec((tt * nbk, cb), lambda b, c, i: (b * nt + i, c))
    vmem = 4 * tt * nbk * cb * 4 + 16 * cb * n_state * 4 + 6 * tt * nbk * n_state * 4
    lane_tile = pltpu.VMEM((V7X_SUBLANES, n_state), F32)
    y, sr, si = pl.pallas_call(
        functools.partial(_s5_body, nbk=nbk),
        grid=(ng, nblk, nt),
        in_specs=[row_spec, lane_spec, lane_spec, b_spec, b_spec, c_spec, c_spec,
                  pl.BlockSpec((1, cb), lambda b, c, i: (0, c)), state_spec, state_spec],
        out_specs=[row_spec, state_spec, state_spec],
        out_shape=[jax.ShapeDtypeStruct((m, d), F32), state, state],
        scratch_shapes=[pltpu.VMEM((tt * nbk, n_state), F32), pltpu.VMEM((tt * nbk, n_state), F32),
                        lane_tile, lane_tile, lane_tile, lane_tile],
        compiler_params=_params(vmem, 3),
        name="s5",
    )(h, abar_r, abar_i, bbr, bbi, ccr, cci, d_skip.reshape(1, d),
      s0r.reshape(ng, nbk, -1), s0i.reshape(ng, nbk, -1))
    return y, sr.reshape(nb, -1), si.reshape(nb, -1)


def _conv_body(b_ref, c_ref, v_ref, w_ref, p_ref, y_ref, s_ref):
    z = c_ref[...] * v_ref[...]
    t = z.shape[0]
    row = lax.broadcasted_iota(jnp.int32, z.shape, 0)
    p0, p1 = p_ref[0:1, :], p_ref[1:2, :]
    z1 = jnp.where(row >= 1, pltpu.roll(z, 1, 0), p1)
    z2 = jnp.where(row >= 2, pltpu.roll(z, 2, 0), jnp.where(row == 0, p0, p1))
    y = w_ref[0:1, :] * z2 + w_ref[1:2, :] * z1 + w_ref[2:3, :] * z
    y_ref[...] = b_ref[...] * y
    s_ref[...] = z[t - (CONV_W - 1):, :]


def _conv_core(bcv, w_conv, prev, nb, t, *, tc=512):
    m, d3 = bcv.shape
    d = d3 // 3
    nc = d // tc
    vmem = 12 * t * tc * 4
    return pl.pallas_call(
        _conv_body,
        grid=(nb, nc),
        in_specs=[
            pl.BlockSpec((t, tc), lambda b, j: (b, j)),
            pl.BlockSpec((t, tc), lambda b, j: (b, nc + j)),
            pl.BlockSpec((t, tc), lambda b, j: (b, 2 * nc + j)),
            pl.BlockSpec((CONV_W, tc), lambda b, j: (0, j)),
            pl.BlockSpec((None, CONV_W - 1, tc), lambda b, j: (b, 0, j)),
        ],
        out_specs=[pl.BlockSpec((t, tc), lambda b, j: (b, j)),
                   pl.BlockSpec((None, CONV_W - 1, tc), lambda b, j: (b, 0, j))],
        out_shape=[jax.ShapeDtypeStruct((m, d), F32), jax.ShapeDtypeStruct((nb, CONV_W - 1, d), F32)],
        compiler_params=_params(vmem, 2),
        name="conv",
    )(bcv, bcv, bcv, w_conv, prev)


def _gla_gate_body(x_ref, g_ref, w1_ref, w2_ref, b_ref, o_ref):
    h = _rms(x_ref[...], g_ref[...]).astype(BF16)
    low = _dot(h, w1_ref[...].astype(BF16))
    y = _dot(low.astype(BF16), w2_ref[...].astype(BF16)) + b_ref[...]
    o_ref[...] = -_softplus(-y) / GLA_TAU


def _gla_gate(x, g, w1, w2, b, *, tm_cap=512):
    m, d = x.shape
    r = w1.shape[1]
    gk = w2.shape[1]
    rp = V7X_LANES
    w1p = jnp.pad(w1, ((0, 0), (0, rp - r)))
    w2p = jnp.pad(w2, ((0, rp - r), (0, 0)))
    tm = _row_tile(m, tm_cap)
    vmem = 3 * tm * d * 4 + 4 * d * rp * 4 + 4 * rp * gk * 4 + 6 * tm * gk * 4
    return pl.pallas_call(
        _gla_gate_body,
        grid=(m // tm,),
        in_specs=[
            pl.BlockSpec((tm, d), lambda i: (i, 0)),
            pl.BlockSpec((1, d), lambda i: (0, 0)),
            pl.BlockSpec((d, rp), lambda i: (0, 0)),
            pl.BlockSpec((rp, gk), lambda i: (0, 0)),
            pl.BlockSpec((1, gk), lambda i: (0, 0)),
        ],
        out_specs=pl.BlockSpec((tm, gk), lambda i: (i, 0)),
        out_shape=jax.ShapeDtypeStruct((m, gk), F32),
        compiler_params=_params(vmem, 1),
        name="gla_gate",
    )(x, g.reshape(1, d), w1p, w2p, b.reshape(1, gk))


def _cumsum_rows(x):
    n = x.shape[0]
    row = lax.broadcasted_iota(jnp.int32, x.shape, 0)
    dist = 1
    while dist < n:
        x = x + jnp.where(row >= dist, pltpu.roll(x, dist, 0), 0.0)
        dist *= 2
    return x


def _gla_chunk(q, k, v, glog, st):
    c, dk = q.shape
    b = _cumsum_rows(glog)
    o = lax.dot_general((q * jnp.exp(b)).astype(BF16), st.astype(BF16), NT_DIMS, preferred_element_type=F32)
    rowc = lax.broadcasted_iota(jnp.int32, (c, 1), 0)
    lane = lax.broadcasted_iota(jnp.int32, (1, c), 1)
    rsub = lax.broadcasted_iota(jnp.int32, (GLA_SUB, 1), 0)
    att_rows = []
    for blk in range(c // GLA_SUB):
        lo = blk * GLA_SUB
        b_i = b[lo:lo + GLA_SUB]
        q_i = q[lo:lo + GLA_SUB]
        if blk > 0:
            b_0 = b[lo - 1:lo]
            q_off = (q_i * jnp.exp(b_i - b_0)).astype(BF16)
            k_off = (k * jnp.exp(jnp.where(rowc < lo, b_0 - b, NEG_INF))).astype(BF16)
            att = lax.dot_general(q_off, k_off, NT_DIMS, preferred_element_type=F32)
        else:
            att = jnp.zeros((GLA_SUB, c), F32)
        for jj in range(GLA_SUB):
            j = lo + jj
            dec = jnp.exp(jnp.where(rsub >= jj, b_i - b[j:j + 1], NEG_INF))
            col = jnp.sum(q_i * k[j:j + 1] * dec, axis=1, keepdims=True)
            att = att + jnp.where(lane == j, col, 0.0)
        att_rows.append(att)
    att = att_rows[0] if len(att_rows) == 1 else jnp.concatenate(att_rows, axis=0)
    vb = v.astype(BF16)
    o = o + _dot(att.astype(BF16), vb)
    b_last = b[c - 1:c]
    k_dec = (k * jnp.exp(b_last - b)).astype(BF16)
    st = st * jnp.exp(b_last) + lax.dot_general(vb, k_dec, TN_DIMS, preferred_element_type=F32)
    return o, st


def _gla_body(*refs, chunk, has_prev, scale):
    refs = list(refs)
    q_ref, k_ref, v_ref, r_ref, gl_ref, ng_ref = refs[:6]
    s0_ref = refs[6] if has_prev else None
    o_ref, s_ref, st_s = refs[-3:]
    t = pl.program_id(2)

    @pl.when(t == 0)
    def _():
        st_s[...] = s0_ref[...].T if has_prev else jnp.zeros_like(st_s)

    def step(ci, carry):
        rows = pl.ds(pl.multiple_of(ci * chunk, chunk), chunk)
        o, st = _gla_chunk(q_ref[rows, :] * scale, k_ref[rows, :], v_ref[rows, :], gl_ref[rows, :], st_s[...])
        st_s[...] = st
        o = o * lax.rsqrt(jnp.mean(o * o, axis=-1, keepdims=True) + EPS) * ng_ref[...]
        o_ref[rows, :] = o * jax.nn.silu(r_ref[rows, :])
        return carry

    lax.fori_loop(0, q_ref.shape[0] // chunk, step, 0)

    @pl.when(t == pl.num_programs(2) - 1)
    def _():
        s_ref[...] = st_s[...].T


def _gla_core(qkvr, glog, norm_g, prev, nb, t, *, tt_cap=256):
    m = qkvr.shape[0]
    gk = glog.shape[1]
    dk = gk // GLA_HEADS
    d = (qkvr.shape[1] - 2 * gk) // 2
    dv = d // GLA_HEADS
    chunk = GLA_CHUNK if t % GLA_CHUNK == 0 else t
    tt = _row_tile(t, tt_cap)
    nt = t // tt
    nqk = gk // dk
    in_specs = [
        pl.BlockSpec((tt, dk), lambda b, h, i: (b * nt + i, h)),
        pl.BlockSpec((tt, dk), lambda b, h, i: (b * nt + i, nqk + h)),
        pl.BlockSpec((tt, dv), lambda b, h, i: (b * nt + i, 2 * gk // dv + h)),
        pl.BlockSpec((tt, dv), lambda b, h, i: (b * nt + i, (2 * gk + d) // dv + h)),
        pl.BlockSpec((tt, dk), lambda b, h, i: (b * nt + i, h)),
        pl.BlockSpec((1, dv), lambda b, h, i: (0, 0)),
    ]
    args = [qkvr, qkvr, qkvr, qkvr, glog, norm_g.reshape(1, dv)]
    state_spec = pl.BlockSpec((None, None, dk, dv), lambda b, h, i: (b, h, 0, 0))
    if prev is not None:
        in_specs.append(state_spec)
        args.append(prev)
    vmem = 2 * tt * (3 * dk + 3 * dv) * 4 + 7 * dk * dv * 4 + 64 * chunk * dv * 4
    return pl.pallas_call(
        functools.partial(_gla_body, chunk=chunk, has_prev=prev is not None, scale=dk ** -0.5),
        grid=(nb, GLA_HEADS, nt),
        in_specs=in_specs,
        out_specs=[pl.BlockSpec((tt, dv), lambda b, h, i: (b * nt + i, h)), state_spec],
        out_shape=[jax.ShapeDtypeStruct((m, d), F32), jax.ShapeDtypeStruct((nb, GLA_HEADS, dk, dv), F32)],
        scratch_shapes=[pltpu.VMEM((dv, dk), F32)],
        compiler_params=_params(vmem, 3),
        name="gla",
    )(*args)


def _trunk(xp, xs, prompt_dims, sample_dims, mem_p, mem_s, p, s5_disc, sample):
    (nbp, tp), (nbs, ts) = prompt_dims, sample_dims
    d = xp.shape[1]
    depth = p["norm_mix"].shape[0]
    new_p, new_s = {}, {}
    for i in range(depth):
        g_mix = p["norm_mix"][i]
        kind = i % 4
        if kind == 0:
            qkv_p, qkv_s = _mm(xp, p["w_sb_qkv"], g=g_mix, small=xs)
            o_p = _sb_attend_prompt(qkv_p, p["sb_bias"], nbp, tp)
            o_s = _sb_attend_sample(qkv_s, p["sb_bias"], sample["cache_sb_k"], sample["cache_sb_v"],
                                    sample["page_table"], nbs, ts)
            xp, xs = _mm(o_p, p["w_sb_out"], res=xp, small=o_s, small_res=xs)
            for new, qkv, nb, t in ((new_p, qkv_p, nbp, tp), (new_s, qkv_s, nbs, ts)):
                shp = (nb, t, SB_HEADS, d // SB_HEADS)
                new["sb"] = (qkv[:, d:2 * d].reshape(shp), qkv[:, 2 * d:].reshape(shp))
        elif kind == 1:
            n_state = d // S5_GROUP_CH * S5_STATE
            zeros = jnp.zeros((nbp, n_state), F32)
            tmaj = (nbp, tp) if V7X_SUBLANES % nbp == 0 else None
            h_p = _rmsnorm(xp, g_mix, time_major=tmaj)
            y_p, sr_p, si_p = _s5_core(h_p, s5_disc, p["s5_d"], zeros, zeros, nbp, tp, nbp if tmaj else 1)
            h_s = _rmsnorm(xs, g_mix)
            y_s, sr_s, si_s = _s5_core(h_s, s5_disc, p["s5_d"], sample["state_s5_re"], sample["state_s5_im"],
                                       nbs, ts, 1)
            xp, xs = _mm(y_p, p["w_s5_glu"], res=xp, glu=True, lhs_time_major=tmaj, small=y_s, small_res=xs)
            for new, sr, si, nb in ((new_p, sr_p, si_p, nbp), (new_s, sr_s, si_s, nbs)):
                sshape = (nb, d // S5_GROUP_CH, S5_STATE)
                new["s5"] = (sr.reshape(sshape), si.reshape(sshape))
        elif kind == 2:
            bcv_p, bcv_s = _mm(xp, p["w_conv_in"], g=g_mix, small=xs)
            yg_p, new_p["conv"] = _conv_core(bcv_p, p["w_conv"], jnp.zeros((nbp, CONV_W - 1, d), F32), nbp, tp)
            yg_s, new_s["conv"] = _conv_core(bcv_s, p["w_conv"], sample["state_conv"], nbs, ts)
            xp, xs = _mm(yg_p, p["w_conv_out"], res=xp, small=yg_s, small_res=xs)
        else:
            qkvr_p, qkvr_s = _mm(xp, p["w_gla_in"], g=g_mix, small=xs)
            gate = (g_mix, p["w_gla_g1"], p["w_gla_g2"], p["b_gla_g"])
            o_p, new_p["gla"] = _gla_core(qkvr_p, _gla_gate(xp, *gate), p["gla_norm"], None, nbp, tp)
            o_s, new_s["gla"] = _gla_core(qkvr_s, _gla_gate(xs, *gate), p["gla_norm"], sample["state_gla"], nbs, ts)
            xp, xs = _mm(o_p, p["w_gla_out"], res=xp, small=o_s, small_res=xs)
        xp = _xattn(xp, p["norm_xattn"][i], p["w_xq"], p["w_xo"], mem_p[0], mem_p[1], i, tp)
        xs = _xattn(xs, p["norm_xattn"][i], p["w_xq"], p["w_xo"], mem_s[0], mem_s[1], i, ts)
        xp, xs = _ffn(xp, p["norm_ffn"][i], p["w_ffn_in"], p["w_ffn_out"], i, small=xs)
    y_p = _rmsnorm(xp, p["norm_final"]).reshape(nbp, tp, d)
    y_s = _rmsnorm(xs, p["norm_final"]).reshape(nbs, ts, d)
    return y_p, y_s, new_p, new_s


def kernel(x_prompt, x_sample, mem_prompt, cache_sb_k, cache_sb_v, page_table, state_s5_re, state_s5_im, state_conv, state_gla, cache_mem_k, cache_mem_v, norm_mix, norm_xattn, norm_mem, norm_ffn, norm_final, w_sb_qkv, w_sb_out, sb_bias, s5_a_re, s5_a_im, s5_log_dt, s5_b_re, s5_b_im, s5_c_re, s5_c_im, s5_d, w_s5_glu, w_conv_in, w_conv, w_conv_out, w_gla_in, w_gla_g1, w_gla_g2, b_gla_g, gla_norm, w_gla_out, w_xq, w_xk, w_xv, w_xo, w_ffn_in, w_ffn_out):
    p = dict(norm_mix=norm_mix, norm_xattn=norm_xattn, norm_ffn=norm_ffn, norm_final=norm_final,
             w_sb_qkv=w_sb_qkv, w_sb_out=w_sb_out, sb_bias=sb_bias, s5_d=s5_d, w_s5_glu=w_s5_glu,
             w_conv_in=w_conv_in, w_conv=w_conv, w_conv_out=w_conv_out,
             w_gla_in=w_gla_in, w_gla_g1=w_gla_g1, w_gla_g2=w_gla_g2, b_gla_g=b_gla_g,
             gla_norm=gla_norm, w_gla_out=w_gla_out, w_xq=w_xq, w_xo=w_xo,
             w_ffn_in=w_ffn_in, w_ffn_out=w_ffn_out)
    nbp, tp, d = x_prompt.shape
    nbs, ts, _ = x_sample.shape
    depth = norm_mix.shape[0]
    n_mem = mem_prompt.shape[1]
    xw = w_xk.shape[-1]

    mem_k_p, mem_v_p = _memkv(mem_prompt.reshape(nbp * n_mem, d), norm_mem, w_xk, w_xv)
    mem_k_p = mem_k_p.reshape(depth, nbp, n_mem, xw)
    mem_v_p = mem_v_p.reshape(depth, nbp, n_mem, xw)
    s5_disc = _s5_discretize(s5_a_re, s5_a_im, s5_log_dt, s5_b_re, s5_b_im, s5_c_re, s5_c_im)

    sample = dict(cache_sb_k=cache_sb_k, cache_sb_v=cache_sb_v, page_table=page_table,
                  state_s5_re=state_s5_re.reshape(nbs, -1), state_s5_im=state_s5_im.reshape(nbs, -1),
                  state_conv=state_conv, state_gla=state_gla)
    mem_s = (cache_mem_k.reshape(depth, nbs, n_mem, xw), cache_mem_v.reshape(depth, nbs, n_mem, xw))
    y_p, y_s, new_p, new_s = _trunk(x_prompt.reshape(nbp * tp, d), x_sample.reshape(nbs * ts, d),
                                    (nbp, tp), (nbs, ts), (mem_k_p, mem_v_p), mem_s, p, s5_disc, sample)

    mshape = (depth, nbp, n_mem, X_HEADS, X_HEAD_DIM)
    return (y_p, y_s, new_p["sb"][0], new_p["sb"][1], new_s["sb"][0], new_s["sb"][1],
            new_p["s5"][0], new_p["s5"][1], new_s["s5"][0], new_s["s5"][1],
            new_p["conv"], new_s["conv"], new_p["gla"], new_s["gla"],
            mem_k_p.reshape(mshape), mem_v_p.reshape(mshape))
```

```python
import functools
import math

import jax
import jax.numpy as jnp
from jax import lax
from jax.experimental import pallas as pl
from jax.experimental.pallas import tpu as pltpu

F32 = jnp.float32
BF16 = jnp.bfloat16
EPS = 1e-6
NEG_INF = float("-inf")
LOG2E = 1.4426950408889634

V7X_VMEM_BYTES = 64 * 1024 * 1024
V7X_SUBLANES = 8
V7X_LANES = 128

SB_HEADS = 16
SB_KEY_TILE = 256
S5_GROUP_CH = 16
S5_STATE = 64
S5_CH_BLOCK = 256
CONV_W = 3
GLA_HEADS = 4
GLA_TAU = 16.0
GLA_CHUNK = 32
GLA_SUB = 8
X_HEADS = 4
X_HEAD_DIM = 128

NT_DIMS = (((1,), (1,)), ((), ()))
TN_DIMS = (((0,), (0,)), ((), ()))


def _params(vmem_bytes, n_grid):
    limit = int(min(V7X_VMEM_BYTES - (6 << 20), max(vmem_bytes * 5 // 4 + (4 << 20), 16 << 20)))
    return pltpu.CompilerParams(dimension_semantics=("arbitrary",) * n_grid, vmem_limit_bytes=limit)


def _rms(x, g):
    return x * lax.rsqrt(jnp.mean(x * x, axis=-1, keepdims=True) + EPS) * g


def _softplus(z):
    return jnp.maximum(z, 0.0) + jnp.log1p(jnp.exp(-jnp.abs(z)))


def _dot(a, b):
    return jnp.dot(a, b, preferred_element_type=F32)


def _split_hi_lo(x):
    hi = x.astype(BF16)
    lo = (x - hi.astype(F32)).astype(BF16)
    return hi, lo


def _row_tile(m, cap):
    t = min(m, cap)
    while m % t:
        t //= 2
    return t


def _mm_body(*refs, norm, mode, dual):
    refs = list(refs)
    x_ref = refs.pop(0)
    g_ref = refs.pop(0) if norm else None
    w_ref = refs.pop(0)
    w2_ref = refs.pop(0) if mode == "glu" else None
    has_res = mode in ("res", "glu")
    res_ref = refs.pop(0) if has_res else None
    xs_ref = refs.pop(0) if dual else None
    ress_ref = refs.pop(0) if dual and has_res else None
    o_ref = refs.pop(0)
    os_ref = refs.pop(0) if dual else None
    lhs = refs.pop(0)
    lhs_s = refs.pop(0) if dual else None
    first_col = pl.program_id(1) == 0

    def load_lhs(src, dst):
        x = src[...]
        if norm:
            x = _rms(x, g_ref[...])
        dst[...] = x.astype(BF16)

    def emit(a, res, out):
        y = _dot(a, w_ref[...].astype(BF16))
        if mode == "glu":
            y = y * jax.nn.sigmoid(_dot(a, w2_ref[...].astype(BF16)))
        if res is not None:
            y = res[...] + y
        out[...] = y

    pl.when(first_col)(lambda: load_lhs(x_ref, lhs))
    emit(lhs[...], res_ref, o_ref)
    if dual:
        @pl.when(pl.program_id(0) == 0)
        def _():
            pl.when(first_col)(lambda: load_lhs(xs_ref, lhs_s))
            emit(lhs_s[...], ress_ref, os_ref)


def _mm(x, w, *, g=None, res=None, glu=False, small=None, small_res=None, tn=512, tm_cap=1024):
    m, k = x.shape
    n = w.shape[1] // (2 if glu else 1)
    nj = n // tn
    mode = "glu" if glu else ("res" if res is not None else "plain")
    dual = small is not None
    tm = _row_tile(m, tm_cap)
    in_specs = [pl.BlockSpec((tm, k), lambda i, j: (i, 0))]
    args = [x]
    if g is not None:
        in_specs.append(pl.BlockSpec((1, k), lambda i, j: (0, 0)))
        args.append(g.reshape(1, k))
    in_specs.append(pl.BlockSpec((k, tn), lambda i, j: (0, j)))
    args.append(w)
    if glu:
        in_specs.append(pl.BlockSpec((k, tn), lambda i, j: (0, j + nj)))
        args.append(w)
    if res is not None:
        in_specs.append(pl.BlockSpec((tm, tn), lambda i, j: (i, j)))
        args.append(res)
    out_specs = [pl.BlockSpec((tm, tn), lambda i, j: (i, j))]
    out_shape = [jax.ShapeDtypeStruct((m, n), F32)]
    scratch = [pltpu.VMEM((tm, k), BF16)]
    ms = 0
    if dual:
        ms = small.shape[0]
        small_cols = pl.BlockSpec((ms, tn), lambda i, j: (0, jnp.where(i == 0, j, nj - 1)))
        in_specs.append(pl.BlockSpec((ms, k), lambda i, j: (0, 0)))
        args.append(small)
        if res is not None:
            in_specs.append(small_cols)
            args.append(small_res)
        out_specs.append(small_cols)
        out_shape.append(jax.ShapeDtypeStruct((ms, n), F32))
        scratch.append(pltpu.VMEM((ms, k), BF16))
    vmem = (2 * (tm + ms) * k * 4 + (tm + ms) * k * 2 + (2 if glu else 1) * 2 * k * tn * 4
            + 4 * (tm + ms) * tn * 4 + 2 * k * tn * 4)
    outs = pl.pallas_call(
        functools.partial(_mm_body, norm=g is not None, mode=mode, dual=dual),
        grid=(m // tm, nj),
        in_specs=in_specs,
        out_specs=out_specs,
        out_shape=out_shape,
        scratch_shapes=scratch,
        compiler_params=_params(vmem, 2),
        name="mm_" + mode,
    )(*args)
    return tuple(outs) if dual else outs[0]


def _ffn_body(*refs, dual):
    if dual:
        x_ref, g_ref, wg_ref, wu_ref, wo_ref, xs_ref, o_ref, os_ref, h_scr, hs_scr = refs
    else:
        x_ref, g_ref, wg_ref, wu_ref, wo_ref, o_ref, h_scr = refs
    first_col = pl.program_id(1) == 0

    def start(src, h_dst, out):
        x = src[...]
        h_dst[...] = _rms(x, g_ref[...]).astype(BF16)
        out[...] = x

    def accumulate(h, out):
        gate = _dot(h, wg_ref[...].astype(BF16))
        up = _dot(h, wu_ref[...].astype(BF16))
        a = (jax.nn.silu(gate) * up).astype(BF16)
        out[...] += _dot(a, wo_ref[...].astype(BF16))

    pl.when(first_col)(lambda: start(x_ref, h_scr, o_ref))
    accumulate(h_scr[...], o_ref)
    if dual:
        @pl.when(pl.program_id(0) == 0)
        def _():
            pl.when(first_col)(lambda: start(xs_ref, hs_scr, os_ref))
            accumulate(hs_scr[...], os_ref)


def _ffn(x, g, w_in, w_out, layer, *, small=None, tf=256, tm_cap=1024):
    m, d = x.shape
    f = w_out.shape[1]
    tm = _row_tile(m, tm_cap)
    nf = f // tf
    dual = small is not None
    ms = small.shape[0] if dual else 0
    in_specs = [
        pl.BlockSpec((tm, d), lambda i, j: (i, 0)),
        pl.BlockSpec((1, d), lambda i, j: (0, 0)),
        pl.BlockSpec((None, d, tf), lambda i, j: (layer, 0, j)),
        pl.BlockSpec((None, d, tf), lambda i, j: (layer, 0, j + nf)),
        pl.BlockSpec((None, tf, d), lambda i, j: (layer, j, 0)),
    ]
    args = [x, g.reshape(1, d), w_in, w_in, w_out]
    out_specs = [pl.BlockSpec((tm, d), lambda i, j: (i, 0))]
    out_shape = [jax.ShapeDtypeStruct((m, d), F32)]
    scratch = [pltpu.VMEM((tm, d), BF16)]
    if dual:
        in_specs.append(pl.BlockSpec((ms, d), lambda i, j: (0, 0)))
        args.append(small)
        out_specs.append(pl.BlockSpec((ms, d), lambda i, j: (0, 0)))
        out_shape.append(jax.ShapeDtypeStruct((ms, d), F32))
        scratch.append(pltpu.VMEM((ms, d), BF16))
    vmem = (4 * (tm + ms) * d * 4 + (tm + ms) * d * 2 + 3 * 2 * d * tf * 4 + 3 * d * tf * 2
            + 3 * (tm + ms) * tf * 4)
    outs = pl.pallas_call(
        functools.partial(_ffn_body, dual=dual),
        grid=(m // tm, nf),
        in_specs=in_specs,
        out_specs=out_specs,
        out_shape=out_shape,
        scratch_shapes=scratch,
        compiler_params=_params(vmem, 2),
        name="ffn",
    )(*args)
    return tuple(outs) if dual else outs[0]


def _xattn_body(x_ref, g_ref, wq_ref, wo_ref, mk_ref, mv_ref, o_ref, wq_s, wo_s):
    @pl.when(pl.program_id(0) == 0)
    def _():
        wq_s[...] = wq_ref[...].astype(BF16)
        wo_s[...] = wo_ref[...].astype(BF16)

    x = x_ref[...]
    h = _rms(x, g_ref[...]).astype(BF16)
    q = _dot(h, wq_s[...])
    mk = mk_ref[...].astype(BF16)
    mv = mv_ref[...].astype(BF16)
    heads = []
    for hh in range(X_HEADS):
        sl = slice(hh * X_HEAD_DIM, (hh + 1) * X_HEAD_DIM)
        s = lax.dot_general(q[:, sl].astype(BF16), mk[:, sl], NT_DIMS, preferred_element_type=F32)
        s = s * X_HEAD_DIM ** -0.5
        e = jnp.exp(s - jnp.max(s, axis=-1, keepdims=True))
        a = e / jnp.sum(e, axis=-1, keepdims=True)
        heads.append(_dot(a.astype(BF16), mv[:, sl]))
    o = jnp.concatenate(heads, axis=1).astype(BF16)
    o_ref[...] = x + _dot(o, wo_s[...])


def _xattn(x, g, w_q, w_o, mem_k, mem_v, layer, rows_per_batch, *, tm_cap=512):
    m, d = x.shape
    xw = w_q.shape[-1]
    n_mem = mem_k.shape[2]
    tm = _row_tile(rows_per_batch, tm_cap)
    tiles_per_batch = rows_per_batch // tm
    vmem = 4 * tm * d * 4 + 2 * 2 * d * xw * 4 + 2 * d * xw * 2 + 8 * n_mem * xw * 4 + 6 * tm * xw * 4
    mem_spec = pl.BlockSpec((None, None, n_mem, xw), lambda i: (layer, i // tiles_per_batch, 0, 0))
    return pl.pallas_call(
        _xattn_body,
        grid=(m // tm,),
        in_specs=[
            pl.BlockSpec((tm, d), lambda i: (i, 0)),
            pl.BlockSpec((1, d), lambda i: (0, 0)),
            pl.BlockSpec((None, d, xw), lambda i: (layer, 0, 0)),
            pl.BlockSpec((None, xw, d), lambda i: (layer, 0, 0)),
            mem_spec,
            mem_spec,
        ],
        out_specs=pl.BlockSpec((tm, d), lambda i: (i, 0)),
        out_shape=jax.ShapeDtypeStruct((m, d), F32),
        scratch_shapes=[pltpu.VMEM((d, xw), BF16), pltpu.VMEM((xw, d), BF16)],
        compiler_params=_params(vmem, 1),
        name="xattn",
    )(x, g.reshape(1, d), w_q, w_o, mem_k, mem_v)


def _memkv_body(m_ref, g_ref, wk_ref, wv_ref, k_ref, v_ref):
    mn = _rms(m_ref[...], g_ref[...]).astype(BF16)
    k_ref[...] = _dot(mn, wk_ref[...].astype(BF16))
    v_ref[...] = _dot(mn, wv_ref[...].astype(BF16))


def _memkv(mem, norm_mem, w_xk, w_xv):
    depth, d, xw = w_xk.shape
    m = mem.shape[0]
    out = jax.ShapeDtypeStruct((depth, m, xw), F32)
    vmem = 2 * m * d * 4 + m * d * 2 + 4 * d * xw * 4 + 2 * d * xw * 2 + 6 * m * xw * 4
    return pl.pallas_call(
        _memkv_body,
        grid=(depth,),
        in_specs=[
            pl.BlockSpec((m, d), lambda l: (0, 0)),
            pl.BlockSpec((None, 1, d), lambda l: (l, 0, 0)),
            pl.BlockSpec((None, d, xw), lambda l: (l, 0, 0)),
            pl.BlockSpec((None, d, xw), lambda l: (l, 0, 0)),
        ],
        out_specs=[pl.BlockSpec((None, m, xw), lambda l: (l, 0, 0))] * 2,
        out_shape=[out, out],
        compiler_params=_params(vmem, 1),
        name="memkv",
    )(mem, norm_mem.reshape(depth, 1, d), w_xk, w_xv)


def _rmsnorm_body(x_ref, g_ref, o_ref):
    o_ref[...] = _rms(x_ref[...], g_ref[...])


def _rmsnorm(x, g, *, tm_cap=512):
    m, d = x.shape
    tm = _row_tile(m, tm_cap)
    return pl.pallas_call(
        _rmsnorm_body,
        grid=(m // tm,),
        in_specs=[pl.BlockSpec((tm, d), lambda i: (i, 0)), pl.BlockSpec((1, d), lambda i: (0, 0))],
        out_specs=pl.BlockSpec((tm, d), lambda i: (i, 0)),
        out_shape=jax.ShapeDtypeStruct((m, d), F32),
        compiler_params=_params(6 * tm * d * 4, 1),
        name="rmsnorm",
    )(x, g.reshape(1, d))


def _suffix_matrix(n):
    r = lax.broadcasted_iota(jnp.int32, (n, n), 0)
    c = lax.broadcasted_iota(jnp.int32, (n, n), 1)
    return jnp.where(r >= c, 1.0, 0.0).astype(BF16)


def _sb_suffix(z, mask, u_mat):
    nz = -z
    lk = jnp.minimum(nz, 0.0) - jnp.log2(1.0 + jnp.exp2(jnp.minimum(z, nz)))
    if mask is not None:
        lk = jnp.where(mask, lk, 0.0)
    hi, lo = _split_hi_lo(lk)
    return _dot(hi, u_mat) + _dot(lo, u_mat)


def _sb_weights(z, incl, mask, acc):
    w = jnp.exp2(z + incl + acc)
    return w if mask is None else jnp.where(mask, w, 0.0)


def _sb_tile(z, mask, acc, u_mat):
    incl = _sb_suffix(z, mask, u_mat)
    return _sb_weights(z, incl, mask, acc), acc + incl[:, 0:1]


def _sbp_body(bias_ref, q_ref, k_ref, v_ref, o_ref, *, tq, tk, scale):
    h = pl.program_id(1)
    i = pl.program_id(2)
    bias = bias_ref[h] * LOG2E
    q = (q_ref[...] * (scale * LOG2E)).astype(BF16)
    u_mat = _suffix_matrix(tk)
    n_diag = tq // tk
    nk = (i + 1) * n_diag

    def group(m, carry, masked):
        acc, out = carry
        starts = [pl.multiple_of((nk - 1 - (n_diag * m + r)) * tk, tk) for r in range(n_diag)]
        zs = [lax.dot_general(q, k_ref[pl.ds(s, tk), :].astype(BF16), NT_DIMS, preferred_element_type=F32) + bias
              for s in starts]
        masks = [None] * n_diag
        if masked:
            qoff = lax.broadcasted_iota(jnp.int32, (tq, tk), 0)
            koff = lax.broadcasted_iota(jnp.int32, (tq, tk), 1)
            masks = [(koff - qoff) < (i * tq - s) for s in starts]
        incls = [_sb_suffix(z, mask, u_mat) for z, mask in zip(zs, masks)]
        for s, z, incl, mask in zip(starts, zs, incls, masks):
            w = _sb_weights(z, incl, mask, acc).astype(BF16)
            acc = acc + incl[:, 0:1]
            out = out + _dot(w, v_ref[pl.ds(s, tk), :].astype(BF16))
        return acc, out

    dh = q_ref.shape[1]
    carry = group(0, (jnp.zeros((tq, 1), F32), jnp.zeros((tq, dh), F32)), True)
    _, out = lax.fori_loop(1, i + 1, lambda m, c: group(m, c, False), carry)
    o_ref[...] = out


def _sb_attend_prompt(qkv, bias, nb, t, *, tq=512):
    m, d3 = qkv.shape
    d = d3 // 3
    dh = d // SB_HEADS
    tq = min(tq, t)
    tk = min(SB_KEY_TILE, tq)
    nq = t // tq
    vmem = 4 * t * dh * 4 + 4 * tq * dh * 4 + 16 * tq * tk * 4
    return pl.pallas_call(
        functools.partial(_sbp_body, tq=tq, tk=tk, scale=dh ** -0.5),
        grid=(nb, SB_HEADS, nq),
        in_specs=[
            pl.BlockSpec(memory_space=pltpu.SMEM),
            pl.BlockSpec((tq, dh), lambda b, h, i: (b * nq + i, h)),
            pl.BlockSpec((t, dh), lambda b, h, i: (b, SB_HEADS + h)),
            pl.BlockSpec((t, dh), lambda b, h, i: (b, 2 * SB_HEADS + h)),
        ],
        out_specs=pl.BlockSpec((tq, dh), lambda b, h, i: (b * nq + i, h)),
        out_shape=jax.ShapeDtypeStruct((m, d), F32),
        compiler_params=_params(vmem, 3),
        name="sb_prompt",
    )(bias, qkv, qkv, qkv)


def _sbs_body(pt_ref, bias_ref, qkv_ref, *refs, t, dh, page, npp, scale):
    del pt_ref
    k_refs, v_refs = refs[:npp], refs[npp:2 * npp]
    o_ref, out_s, acc_s = refs[2 * npp:]
    s = pl.program_id(1)
    d = SB_HEADS * dh
    rows = SB_HEADS * t
    u_mat = _suffix_matrix(page)

    def attend(pages, mask, acc, outs):
        qs = [(qkv_ref[:, h * dh:(h + 1) * dh] * (scale * LOG2E)).astype(BF16) for h in range(SB_HEADS)]
        zs = [jnp.concatenate([lax.dot_general(qs[h], get_k(h), NT_DIMS, preferred_element_type=F32)
                               + bias_ref[h] * LOG2E for h in range(SB_HEADS)], axis=0)
              for get_k, _ in pages]
        incls = [_sb_suffix(z, mask, u_mat) for z in zs]
        ws = []
        for z, incl in zip(zs, incls):
            ws.append(_sb_weights(z, incl, mask, acc))
            acc = acc + incl[:, 0:1]
        for w, (_, get_v) in zip(ws, pages):
            outs = [outs[h] + _dot(w[h * t:(h + 1) * t, :].astype(BF16), get_v(h)) for h in range(SB_HEADS)]
        return acc, outs

    def store(acc, outs):
        acc_s[...] = acc
        for h in range(SB_HEADS):
            out_s[h * t:(h + 1) * t, :] = outs[h]

    @pl.when(s == 0)
    def _():
        pad = jnp.zeros((page - t, dh), F32)
        key = lax.broadcasted_iota(jnp.int32, (rows, page), 1)
        qry = lax.broadcasted_iota(jnp.int32, (rows, page), 0) % t
        new_tokens = (
            lambda h: jnp.concatenate([qkv_ref[:, d + h * dh:d + (h + 1) * dh], pad], axis=0).astype(BF16),
            lambda h: jnp.concatenate([qkv_ref[:, 2 * d + h * dh:2 * d + (h + 1) * dh], pad], axis=0).astype(BF16))
        acc, outs = attend([new_tokens], key < qry, jnp.zeros((rows, 1), F32),
                           [jnp.zeros((t, dh), F32)] * SB_HEADS)
        store(acc, outs)

    def cached_page(j):
        return (lambda h: k_refs[j][pl.ds(h, page, stride=SB_HEADS), :].astype(BF16),
                lambda h: v_refs[j][pl.ds(h, page, stride=SB_HEADS), :].astype(BF16))

    acc, outs = attend([cached_page(j) for j in range(npp)], None, acc_s[...],
                       [out_s[h * t:(h + 1) * t, :] for h in range(SB_HEADS)])
    store(acc, outs)

    @pl.when(s == pl.num_programs(1) - 1)
    def _():
        for h in range(SB_HEADS):
            o_ref[:, h * dh:(h + 1) * dh] = outs[h]


def _sb_attend_sample(qkv, bias, cache_k, cache_v, page_table, nb, t, *, npp=4):
    m, d3 = qkv.shape
    d = d3 // 3
    dh = d // SB_HEADS
    n_phys, page = cache_k.shape[0], cache_k.shape[1]
    n_pages = page_table.shape[1]
    while n_pages % npp:
        npp //= 2
    ck = cache_k.reshape(n_phys, page * SB_HEADS, dh)
    cv = cache_v.reshape(n_phys, page * SB_HEADS, dh)

    def cache_spec(j):
        return pl.BlockSpec((None, page * SB_HEADS, dh),
                            lambda b, s, pt: (pt[b, n_pages - 1 - (s * npp + j)], 0, 0))

    vmem = 4 * npp * page * d * 4 + 2 * t * d3 * 4 + 24 * npp * SB_HEADS * t * page * 4
    return pl.pallas_call(
        functools.partial(_sbs_body, t=t, dh=dh, page=page, npp=npp, scale=dh ** -0.5),
        grid_spec=pltpu.PrefetchScalarGridSpec(
            num_scalar_prefetch=1,
            grid=(nb, n_pages // npp),
            in_specs=[pl.BlockSpec(memory_space=pltpu.SMEM), pl.BlockSpec((t, d3), lambda b, s, pt: (b, 0))]
            + [cache_spec(j) for j in range(npp)] * 2,
            out_specs=pl.BlockSpec((t, d), lambda b, s, pt: (b, 0)),
            scratch_shapes=[pltpu.VMEM((SB_HEADS * t, dh), F32), pltpu.VMEM((SB_HEADS * t, 1), F32)],
        ),
        out_shape=jax.ShapeDtypeStruct((m, d), F32),
        compiler_params=_params(vmem, 2),
        name="sb_sample",
    )(page_table, bias, qkv, *([ck] * npp), *([cv] * npp))


def _cmul(ar, ai, br, bi):
    return ar * br - ai * bi, ar * bi + ai * br


def _gelu_tanh(x):
    return 0.5 * x * (1.0 + jnp.tanh(math.sqrt(2.0 / math.pi) * (x + 0.044715 * (x * x * x))))


def _s5_body(h_ref, ar_ref, ai_ref, br_ref, bi_ref, cr_ref, ci_ref, d_ref, s0r_ref, s0i_ref,
             y_ref, sr_ref, si_ref, xr_s, xi_s, pr_s, pi_s, cr_s, ci_s, *, nbk):
    t = pl.program_id(2)
    rows8, n_state = pr_s.shape
    row = lax.broadcasted_iota(jnp.int32, (rows8, n_state), 0)

    def last_step_to_all_rows(x):
        x = jnp.where(row >= rows8 - nbk, x, 0.0)
        have = nbk
        while have < rows8:
            x = x + pltpu.roll(x, rows8 - have, 0)
            have *= 2
        return x

    @pl.when(t == 0)
    def _():
        ar, ai = ar_ref[...], ai_ref[...]
        pr, pi = ar, ai
        for r in range(rows8):
            if r and r % nbk == 0:
                pr, pi = _cmul(pr, pi, ar, ai)
            pr_s[r:r + 1, :] = pr
            pi_s[r:r + 1, :] = pi
        cr_s[...] = jnp.zeros_like(cr_s)
        ci_s[...] = jnp.zeros_like(ci_s)
        cr_s[rows8 - nbk:, :] = s0r_ref[...]
        ci_s[rows8 - nbk:, :] = s0i_ref[...]
        cr_s[...] = last_step_to_all_rows(cr_s[...])
        ci_s[...] = last_step_to_all_rows(ci_s[...])

    tt = h_ref.shape[1]
    n_slab = xr_s.shape[0]
    lanes = [slice(s * V7X_LANES, (s + 1) * V7X_LANES) for s in range(n_slab)]

    def batch_rows(b):
        return pl.ds(b, tt, stride=nbk) if nbk > 1 else pl.ds(0, tt)

    b_r, b_i = br_ref[...].astype(BF16), bi_ref[...].astype(BF16)
    for b in range(nbk):
        u = h_ref[b].astype(BF16)
        bu_r, bu_i = _dot(u, b_r), _dot(u, b_i)
        for s in range(n_slab):
            xr_s[s, batch_rows(b), :] = bu_r[:, lanes[s]]
            xi_s[s, batch_rows(b), :] = bu_i[:, lanes[s]]
    pw_r, pw_i = pr_s[...], pi_s[...]

    def tile(blk, carry):
        c_r, c_i = carry
        rows = pl.ds(pl.multiple_of(blk * rows8, rows8), rows8)
        x_r = jnp.concatenate([xr_s[s, rows, :] for s in range(n_slab)], axis=1)
        x_i = jnp.concatenate([xi_s[s, rows, :] for s in range(n_slab)], axis=1)
        dist = 1
        while dist * nbk < rows8:
            lo = (dist - 1) * nbk
            a_r, a_i = pw_r[lo:lo + 1, :], pw_i[lo:lo + 1, :]
            keep = row >= dist * nbk
            sh_r = jnp.where(keep, pltpu.roll(x_r, dist * nbk, 0), 0.0)
            sh_i = jnp.where(keep, pltpu.roll(x_i, dist * nbk, 0), 0.0)
            d_r, d_i = _cmul(a_r, a_i, sh_r, sh_i)
            x_r, x_i = x_r + d_r, x_i + d_i
            dist *= 2
        d_r, d_i = _cmul(pw_r, pw_i, c_r, c_i)
        x_r, x_i = x_r + d_r, x_i + d_i
        for s in range(n_slab):
            xr_s[s, rows, :] = x_r[:, lanes[s]]
            xi_s[s, rows, :] = x_i[:, lanes[s]]
        return last_step_to_all_rows(x_r), last_step_to_all_rows(x_i)

    c_r, c_i = lax.fori_loop(0, xr_s.shape[1] // rows8, tile, (cr_s[...], ci_s[...]))
    cr_s[...] = c_r
    ci_s[...] = c_i

    c_re, c_im = cr_ref[...].astype(BF16), ci_ref[...].astype(BF16)
    for b in range(nbk):
        st_r = jnp.concatenate([xr_s[s, batch_rows(b), :] for s in range(n_slab)], axis=1).astype(BF16)
        st_i = jnp.concatenate([xi_s[s, batch_rows(b), :] for s in range(n_slab)], axis=1).astype(BF16)
        y = _dot(st_r, c_re) - _dot(st_i, c_im)
        y_ref[b] = _gelu_tanh(y + d_ref[...] * h_ref[b])

    @pl.when(t == pl.num_programs(2) - 1)
    def _():
        sr_ref[...] = c_r[rows8 - nbk:, :]
        si_ref[...] = c_i[rows8 - nbk:, :]


def _s5_discretize(a_re, a_im, log_dt, b_re, b_im, c_re, c_im):
    g, p = a_re.shape
    gb = S5_CH_BLOCK // S5_GROUP_CH
    nblk = g // gb
    dt = jnp.exp(log_dt)[:, None]
    mag = jnp.exp(a_re * dt)
    abar_r, abar_i = mag * jnp.cos(a_im * dt), mag * jnp.sin(a_im * dt)
    xr, xi = abar_r - 1.0, abar_i
    den = a_re * a_re + a_im * a_im
    coef_r = (xr * a_re + xi * a_im) / den
    coef_i = (xi * a_re - xr * a_im) / den
    bbar_r = coef_r[..., None] * b_re - coef_i[..., None] * b_im
    bbar_i = coef_r[..., None] * b_im + coef_i[..., None] * b_re
    eye = jnp.eye(gb, dtype=F32)

    def b_big(bb):
        blk = bb.reshape(nblk, gb, p, S5_GROUP_CH).transpose(0, 1, 3, 2)
        return blk[:, :, :, None, :] * eye[None, :, None, :, None]

    def c_big(cc):
        blk = cc.reshape(nblk, gb, S5_GROUP_CH, p).transpose(0, 1, 3, 2)
        return blk[:, :, :, None, :] * eye[None, :, None, :, None]

    n_state = gb * p
    return (abar_r.reshape(1, g * p), abar_i.reshape(1, g * p),
            b_big(bbar_r).reshape(nblk, S5_CH_BLOCK, n_state), b_big(bbar_i).reshape(nblk, S5_CH_BLOCK, n_state),
            c_big(c_re).reshape(nblk, n_state, S5_CH_BLOCK), c_big(c_im).reshape(nblk, n_state, S5_CH_BLOCK))


def _s5_core(h, disc, d_skip, s0r, s0i, nb, t, *, tt_cap=256):
    m, d = h.shape
    nbk = math.gcd(nb, V7X_SUBLANES)
    abar_r, abar_i, bbr, bbi, ccr, cci = disc
    nblk, cb, n_state = bbr.shape
    tt = _row_tile(t, tt_cap)
    nt = t // tt
    ng = nb // nbk
    state = jax.ShapeDtypeStruct((ng, nbk, nblk * n_state), F32)
    lane_spec = pl.BlockSpec((1, n_state), lambda b, c, i: (0, c))
    state_spec = pl.BlockSpec((None, nbk, n_state), lambda b, c, i: (b, 0, c))
    b_spec = pl.BlockSpec((None, cb, n_state), lambda b, c, i: (c, 0, 0))
    c_spec = pl.BlockSpec((None, n_state, cb), lambda b, c, i: (c, 0, 0))
    row_spec = pl.BlockSpec((nbk, tt, cb), lambda b, c, i: (b, i, c))
    vmem = 4 * tt * nbk * cb * 4 + 16 * cb * n_state * 4 + 8 * tt * nbk * n_state * 4
    lane_tile = pltpu.VMEM((V7X_SUBLANES, n_state), F32)
    slabs = pltpu.VMEM((n_state // V7X_LANES, tt * nbk, V7X_LANES), F32)
    y, sr, si = pl.pallas_call(
        functools.partial(_s5_body, nbk=nbk),
        grid=(ng, nblk, nt),
        in_specs=[row_spec, lane_spec, lane_spec, b_spec, b_spec, c_spec, c_spec,
                  pl.BlockSpec((1, cb), lambda b, c, i: (0, c)), state_spec, state_spec],
        out_specs=[row_spec, state_spec, state_spec],
        out_shape=[jax.ShapeDtypeStruct((nb, t, d), F32), state, state],
        scratch_shapes=[slabs, slabs, lane_tile, lane_tile, lane_tile, lane_tile],
        compiler_params=_params(vmem, 3),
        name="s5",
    )(h.reshape(nb, t, d), abar_r, abar_i, bbr, bbi, ccr, cci, d_skip.reshape(1, d),
      s0r.reshape(ng, nbk, -1), s0i.reshape(ng, nbk, -1))
    return y.reshape(m, d), sr.reshape(nb, -1), si.reshape(nb, -1)


def _conv_body(b_ref, c_ref, v_ref, w_ref, p_ref, y_ref, s_ref):
    z = c_ref[...] * v_ref[...]
    t = z.shape[0]
    row = lax.broadcasted_iota(jnp.int32, z.shape, 0)
    p0, p1 = p_ref[0:1, :], p_ref[1:2, :]
    z1 = jnp.where(row >= 1, pltpu.roll(z, 1, 0), p1)
    z2 = jnp.where(row >= 2, pltpu.roll(z, 2, 0), jnp.where(row == 0, p0, p1))
    y = w_ref[0:1, :] * z2 + w_ref[1:2, :] * z1 + w_ref[2:3, :] * z
    y_ref[...] = b_ref[...] * y
    s_ref[...] = z[t - (CONV_W - 1):, :]


def _conv_core(bcv, w_conv, prev, nb, t, *, tc=512):
    m, d3 = bcv.shape
    d = d3 // 3
    nc = d // tc
    vmem = 12 * t * tc * 4
    return pl.pallas_call(
        _conv_body,
        grid=(nb, nc),
        in_specs=[
            pl.BlockSpec((t, tc), lambda b, j: (b, j)),
            pl.BlockSpec((t, tc), lambda b, j: (b, nc + j)),
            pl.BlockSpec((t, tc), lambda b, j: (b, 2 * nc + j)),
            pl.BlockSpec((CONV_W, tc), lambda b, j: (0, j)),
            pl.BlockSpec((None, CONV_W - 1, tc), lambda b, j: (b, 0, j)),
        ],
        out_specs=[pl.BlockSpec((t, tc), lambda b, j: (b, j)),
                   pl.BlockSpec((None, CONV_W - 1, tc), lambda b, j: (b, 0, j))],
        out_shape=[jax.ShapeDtypeStruct((m, d), F32), jax.ShapeDtypeStruct((nb, CONV_W - 1, d), F32)],
        compiler_params=_params(vmem, 2),
        name="conv",
    )(bcv, bcv, bcv, w_conv, prev)


def _gla_gate_body(x_ref, g_ref, w1_ref, w2_ref, b_ref, o_ref):
    h = _rms(x_ref[...], g_ref[...]).astype(BF16)
    low = _dot(h, w1_ref[...].astype(BF16))
    y = _dot(low.astype(BF16), w2_ref[...].astype(BF16)) + b_ref[...]
    o_ref[...] = -_softplus(-y) / GLA_TAU


def _gla_gate(x, g, w1, w2, b, *, tm_cap=512):
    m, d = x.shape
    r = w1.shape[1]
    gk = w2.shape[1]
    rp = V7X_LANES
    w1p = jnp.pad(w1, ((0, 0), (0, rp - r)))
    w2p = jnp.pad(w2, ((0, rp - r), (0, 0)))
    tm = _row_tile(m, tm_cap)
    vmem = 3 * tm * d * 4 + 4 * d * rp * 4 + 4 * rp * gk * 4 + 6 * tm * gk * 4
    return pl.pallas_call(
        _gla_gate_body,
        grid=(m // tm,),
        in_specs=[
            pl.BlockSpec((tm, d), lambda i: (i, 0)),
            pl.BlockSpec((1, d), lambda i: (0, 0)),
            pl.BlockSpec((d, rp), lambda i: (0, 0)),
            pl.BlockSpec((rp, gk), lambda i: (0, 0)),
            pl.BlockSpec((1, gk), lambda i: (0, 0)),
        ],
        out_specs=pl.BlockSpec((tm, gk), lambda i: (i, 0)),
        out_shape=jax.ShapeDtypeStruct((m, gk), F32),
        compiler_params=_params(vmem, 1),
        name="gla_gate",
    )(x, g.reshape(1, d), w1p, w2p, b.reshape(1, gk))


def _cumsum_rows(x):
    n = x.shape[0]
    row = lax.broadcasted_iota(jnp.int32, x.shape, 0)
    dist = 1
    while dist < n:
        x = x + jnp.where(row >= dist, pltpu.roll(x, dist, 0), 0.0)
        dist *= 2
    return x


def _gla_chunk(q, k, v, glog, st):
    c, dk = q.shape
    b = _cumsum_rows(glog)
    o = lax.dot_general((q * jnp.exp(b)).astype(BF16), st.astype(BF16), NT_DIMS, preferred_element_type=F32)
    rowc = lax.broadcasted_iota(jnp.int32, (c, 1), 0)
    lane = lax.broadcasted_iota(jnp.int32, (1, c), 1)
    rsub = lax.broadcasted_iota(jnp.int32, (GLA_SUB, 1), 0)
    att_rows = []
    for blk in range(c // GLA_SUB):
        lo = blk * GLA_SUB
        b_i = b[lo:lo + GLA_SUB]
        q_i = q[lo:lo + GLA_SUB]
        if blk > 0:
            b_0 = b[lo - 1:lo]
            q_off = (q_i * jnp.exp(b_i - b_0)).astype(BF16)
            k_off = (k * jnp.exp(jnp.where(rowc < lo, b_0 - b, NEG_INF))).astype(BF16)
            att = lax.dot_general(q_off, k_off, NT_DIMS, preferred_element_type=F32)
        else:
            att = jnp.zeros((GLA_SUB, c), F32)
        for jj in range(GLA_SUB):
            j = lo + jj
            dec = jnp.exp(jnp.where(rsub >= jj, b_i - b[j:j + 1], NEG_INF))
            col = jnp.sum(q_i * k[j:j + 1] * dec, axis=1, keepdims=True)
            att = att + jnp.where(lane == j, col, 0.0)
        att_rows.append(att)
    att = att_rows[0] if len(att_rows) == 1 else jnp.concatenate(att_rows, axis=0)
    vb = v.astype(BF16)
    o = o + _dot(att.astype(BF16), vb)
    b_last = b[c - 1:c]
    k_dec = (k * jnp.exp(b_last - b)).astype(BF16)
    st = st * jnp.exp(b_last) + lax.dot_general(vb, k_dec, TN_DIMS, preferred_element_type=F32)
    return o, st


def _gla_body(*refs, chunk, has_prev, scale):
    refs = list(refs)
    q_ref, k_ref, v_ref, r_ref, gl_ref, ng_ref = refs[:6]
    s0_ref = refs[6] if has_prev else None
    o_ref, s_ref, st_s = refs[-3:]
    t = pl.program_id(2)

    @pl.when(t == 0)
    def _():
        st_s[...] = s0_ref[...].T if has_prev else jnp.zeros_like(st_s)

    def step(ci, carry):
        rows = pl.ds(pl.multiple_of(ci * chunk, chunk), chunk)
        o, st = _gla_chunk(q_ref[rows, :] * scale, k_ref[rows, :], v_ref[rows, :], gl_ref[rows, :], st_s[...])
        st_s[...] = st
        o = o * lax.rsqrt(jnp.mean(o * o, axis=-1, keepdims=True) + EPS) * ng_ref[...]
        o_ref[rows, :] = o * jax.nn.silu(r_ref[rows, :])
        return carry

    lax.fori_loop(0, q_ref.shape[0] // chunk, step, 0)

    @pl.when(t == pl.num_programs(2) - 1)
    def _():
        s_ref[...] = st_s[...].T


def _gla_core(qkvr, glog, norm_g, prev, nb, t, *, tt_cap=256):
    m = qkvr.shape[0]
    gk = glog.shape[1]
    dk = gk // GLA_HEADS
    d = (qkvr.shape[1] - 2 * gk) // 2
    dv = d // GLA_HEADS
    chunk = GLA_CHUNK if t % GLA_CHUNK == 0 else t
    tt = _row_tile(t, tt_cap)
    nt = t // tt
    nqk = gk // dk
    in_specs = [
        pl.BlockSpec((tt, dk), lambda b, h, i: (b * nt + i, h)),
        pl.BlockSpec((tt, dk), lambda b, h, i: (b * nt + i, nqk + h)),
        pl.BlockSpec((tt, dv), lambda b, h, i: (b * nt + i, 2 * gk // dv + h)),
        pl.BlockSpec((tt, dv), lambda b, h, i: (b * nt + i, (2 * gk + d) // dv + h)),
        pl.BlockSpec((tt, dk), lambda b, h, i: (b * nt + i, h)),
        pl.BlockSpec((1, dv), lambda b, h, i: (0, 0)),
    ]
    args = [qkvr, qkvr, qkvr, qkvr, glog, norm_g.reshape(1, dv)]
    state_spec = pl.BlockSpec((None, None, dk, dv), lambda b, h, i: (b, h, 0, 0))
    if prev is not None:
        in_specs.append(state_spec)
        args.append(prev)
    vmem = 2 * tt * (3 * dk + 3 * dv) * 4 + 7 * dk * dv * 4 + 64 * chunk * dv * 4
    return pl.pallas_call(
        functools.partial(_gla_body, chunk=chunk, has_prev=prev is not None, scale=dk ** -0.5),
        grid=(nb, GLA_HEADS, nt),
        in_specs=in_specs,
        out_specs=[pl.BlockSpec((tt, dv), lambda b, h, i: (b * nt + i, h)), state_spec],
        out_shape=[jax.ShapeDtypeStruct((m, d), F32), jax.ShapeDtypeStruct((nb, GLA_HEADS, dk, dv), F32)],
        scratch_shapes=[pltpu.VMEM((dv, dk), F32)],
        compiler_params=_params(vmem, 3),
        name="gla",
    )(*args)


def _trunk(xp, xs, prompt_dims, sample_dims, mem_p, mem_s, p, s5_disc, sample):
    (nbp, tp), (nbs, ts) = prompt_dims, sample_dims
    d = xp.shape[1]
    depth = p["norm_mix"].shape[0]
    new_p, new_s = {}, {}
    for i in range(depth):
        g_mix = p["norm_mix"][i]
        kind = i % 4
        if kind == 0:
            qkv_p, qkv_s = _mm(xp, p["w_sb_qkv"], g=g_mix, small=xs)
            o_p = _sb_attend_prompt(qkv_p, p["sb_bias"], nbp, tp)
            o_s = _sb_attend_sample(qkv_s, p["sb_bias"], sample["cache_sb_k"], sample["cache_sb_v"],
                                    sample["page_table"], nbs, ts)
            xp, xs = _mm(o_p, p["w_sb_out"], res=xp, small=o_s, small_res=xs)
            for new, qkv, nb, t in ((new_p, qkv_p, nbp, tp), (new_s, qkv_s, nbs, ts)):
                shp = (nb, t, SB_HEADS, d // SB_HEADS)
                new["sb"] = (qkv[:, d:2 * d].reshape(shp), qkv[:, 2 * d:].reshape(shp))
        elif kind == 1:
            n_state = d // S5_GROUP_CH * S5_STATE
            zeros = jnp.zeros((nbp, n_state), F32)
            y_p, sr_p, si_p = _s5_core(_rmsnorm(xp, g_mix), s5_disc, p["s5_d"], zeros, zeros, nbp, tp)
            y_s, sr_s, si_s = _s5_core(_rmsnorm(xs, g_mix), s5_disc, p["s5_d"], sample["state_s5_re"],
                                       sample["state_s5_im"], nbs, ts)
            xp, xs = _mm(y_p, p["w_s5_glu"], res=xp, glu=True, small=y_s, small_res=xs)
            for new, sr, si, nb in ((new_p, sr_p, si_p, nbp), (new_s, sr_s, si_s, nbs)):
                sshape = (nb, d // S5_GROUP_CH, S5_STATE)
                new["s5"] = (sr.reshape(sshape), si.reshape(sshape))
        elif kind == 2:
            bcv_p, bcv_s = _mm(xp, p["w_conv_in"], g=g_mix, small=xs)
            yg_p, new_p["conv"] = _conv_core(bcv_p, p["w_conv"], jnp.zeros((nbp, CONV_W - 1, d), F32), nbp, tp)
            yg_s, new_s["conv"] = _conv_core(bcv_s, p["w_conv"], sample["state_conv"], nbs, ts)
            xp, xs = _mm(yg_p, p["w_conv_out"], res=xp, small=yg_s, small_res=xs)
        else:
            qkvr_p, qkvr_s = _mm(xp, p["w_gla_in"], g=g_mix, small=xs)
            gate = (g_mix, p["w_gla_g1"], p["w_gla_g2"], p["b_gla_g"])
            o_p, new_p["gla"] = _gla_core(qkvr_p, _gla_gate(xp, *gate), p["gla_norm"], None, nbp, tp)
            o_s, new_s["gla"] = _gla_core(qkvr_s, _gla_gate(xs, *gate), p["gla_norm"], sample["state_gla"], nbs, ts)
            xp, xs = _mm(o_p, p["w_gla_out"], res=xp, small=o_s, small_res=xs)
        xp = _xattn(xp, p["norm_xattn"][i], p["w_xq"], p["w_xo"], mem_p[0], mem_p[1], i, tp)
        xs = _xattn(xs, p["norm_xattn"][i], p["w_xq"], p["w_xo"], mem_s[0], mem_s[1], i, ts)
        xp, xs = _ffn(xp, p["norm_ffn"][i], p["w_ffn_in"], p["w_ffn_out"], i, small=xs)
    y_p = _rmsnorm(xp, p["norm_final"]).reshape(nbp, tp, d)
    y_s = _rmsnorm(xs, p["norm_final"]).reshape(nbs, ts, d)
    return y_p, y_s, new_p, new_s


def kernel(x_prompt, x_sample, mem_prompt, cache_sb_k, cache_sb_v, page_table, state_s5_re, state_s5_im, state_conv, state_gla, cache_mem_k, cache_mem_v, norm_mix, norm_xattn, norm_mem, norm_ffn, norm_final, w_sb_qkv, w_sb_out, sb_bias, s5_a_re, s5_a_im, s5_log_dt, s5_b_re, s5_b_im, s5_c_re, s5_c_im, s5_d, w_s5_glu, w_conv_in, w_conv, w_conv_out, w_gla_in, w_gla_g1, w_gla_g2, b_gla_g, gla_norm, w_gla_out, w_xq, w_xk, w_xv, w_xo, w_ffn_in, w_ffn_out):
    p = dict(norm_mix=norm_mix, norm_xattn=norm_xattn, norm_ffn=norm_ffn, norm_final=norm_final,
             w_sb_qkv=w_sb_qkv, w_sb_out=w_sb_out, sb_bias=sb_bias, s5_d=s5_d, w_s5_glu=w_s5_glu,
             w_conv_in=w_conv_in, w_conv=w_conv, w_conv_out=w_conv_out,
             w_gla_in=w_gla_in, w_gla_g1=w_gla_g1, w_gla_g2=w_gla_g2, b_gla_g=b_gla_g,
             gla_norm=gla_norm, w_gla_out=w_gla_out, w_xq=w_xq, w_xo=w_xo,
             w_ffn_in=w_ffn_in, w_ffn_out=w_ffn_out)
    nbp, tp, d = x_prompt.shape
    nbs, ts, _ = x_sample.shape
    depth = norm_mix.shape[0]
    n_mem = mem_prompt.shape[1]
    xw = w_xk.shape[-1]

    mem_k_p, mem_v_p = _memkv(mem_prompt.reshape(nbp * n_mem, d), norm_mem, w_xk, w_xv)
    mem_k_p = mem_k_p.reshape(depth, nbp, n_mem, xw)
    mem_v_p = mem_v_p.reshape(depth, nbp, n_mem, xw)
    s5_disc = _s5_discretize(s5_a_re, s5_a_im, s5_log_dt, s5_b_re, s5_b_im, s5_c_re, s5_c_im)

    sample = dict(cache_sb_k=cache_sb_k, cache_sb_v=cache_sb_v, page_table=page_table,
                  state_s5_re=state_s5_re.reshape(nbs, -1), state_s5_im=state_s5_im.reshape(nbs, -1),
                  state_conv=state_conv, state_gla=state_gla)
    mem_s = (cache_mem_k.reshape(depth, nbs, n_mem, xw), cache_mem_v.reshape(depth, nbs, n_mem, xw))
    y_p, y_s, new_p, new_s = _trunk(x_prompt.reshape(nbp * tp, d), x_sample.reshape(nbs * ts, d),
                                    (nbp, tp), (nbs, ts), (mem_k_p, mem_v_p), mem_s, p, s5_disc, sample)

    mshape = (depth, nbp, n_mem, X_HEADS, X_HEAD_DIM)
    return (y_p, y_s, new_p["sb"][0], new_p["sb"][1], new_s["sb"][0], new_s["sb"][1],
            new_p["s5"][0], new_p["s5"][1], new_s["s5"][0], new_s["s5"][1],
            new_p["conv"], new_s["conv"], new_p["gla"], new_s["gla"],
            mem_k_p.reshape(mshape), mem_v_p.reshape(mshape))
```

```python
import functools
import math

import jax
import jax.numpy as jnp
from jax import lax
from jax.experimental import pallas as pl
from jax.experimental.pallas import tpu as pltpu

F32 = jnp.float32
BF16 = jnp.bfloat16
EPS = 1e-6
NEG_INF = float("-inf")
LOG2E = 1.4426950408889634

V7X_VMEM_BYTES = 64 * 1024 * 1024
V7X_SUBLANES = 8
V7X_LANES = 128

SB_HEADS = 16
SB_KEY_TILE = 256
S5_GROUP_CH = 16
S5_STATE = 64
S5_CH_BLOCK = 256
CONV_W = 3
GLA_HEADS = 4
GLA_TAU = 16.0
GLA_CHUNK = 32
GLA_SUB = 8
X_HEADS = 4
X_HEAD_DIM = 128

NT_DIMS = (((1,), (1,)), ((), ()))
TN_DIMS = (((0,), (0,)), ((), ()))


def _params(vmem_bytes, n_grid):
    limit = int(min(V7X_VMEM_BYTES - (6 << 20), max(vmem_bytes * 5 // 4 + (4 << 20), 16 << 20)))
    return pltpu.CompilerParams(dimension_semantics=("arbitrary",) * n_grid, vmem_limit_bytes=limit)


def _rms(x, g):
    return x * lax.rsqrt(jnp.mean(x * x, axis=-1, keepdims=True) + EPS) * g


def _softplus(z):
    return jnp.maximum(z, 0.0) + jnp.log1p(jnp.exp(-jnp.abs(z)))


def _dot(a, b):
    return jnp.dot(a, b, preferred_element_type=F32)


def _split_hi_lo(x):
    hi = x.astype(BF16)
    lo = (x - hi.astype(F32)).astype(BF16)
    return hi, lo


def _mxu_feed_dtype(rows_per_store):
    return BF16 if rows_per_store % (2 * V7X_SUBLANES) == 0 else F32


def _row_tile(m, cap):
    t = min(m, cap)
    while m % t:
        t //= 2
    return t


def _mm_body(*refs, norm, mode, dual):
    refs = list(refs)
    x_ref = refs.pop(0)
    g_ref = refs.pop(0) if norm else None
    w_ref = refs.pop(0)
    w2_ref = refs.pop(0) if mode == "glu" else None
    has_res = mode in ("res", "glu")
    res_ref = refs.pop(0) if has_res else None
    xs_ref = refs.pop(0) if dual else None
    ress_ref = refs.pop(0) if dual and has_res else None
    o_ref = refs.pop(0)
    os_ref = refs.pop(0) if dual else None
    lhs = refs.pop(0)
    lhs_s = refs.pop(0) if dual else None
    first_col = pl.program_id(1) == 0

    def load_lhs(src, dst):
        x = src[...]
        if norm:
            x = _rms(x, g_ref[...])
        dst[...] = x.astype(BF16)

    def emit(a, res, out):
        y = _dot(a, w_ref[...].astype(BF16))
        if mode == "glu":
            y = y * jax.nn.sigmoid(_dot(a, w2_ref[...].astype(BF16)))
        if res is not None:
            y = res[...] + y
        out[...] = y

    pl.when(first_col)(lambda: load_lhs(x_ref, lhs))
    emit(lhs[...], res_ref, o_ref)
    if dual:
        @pl.when(pl.program_id(0) == 0)
        def _():
            pl.when(first_col)(lambda: load_lhs(xs_ref, lhs_s))
            emit(lhs_s[...], ress_ref, os_ref)


def _mm(x, w, *, g=None, res=None, glu=False, small=None, small_res=None, tm_cap=1024):
    m, k = x.shape
    n = w.shape[1] // (2 if glu else 1)
    mode = "glu" if glu else ("res" if res is not None else "plain")
    tn = 1024 if (mode == "plain" and n % 1024 == 0) else 512
    nj = n // tn
    dual = small is not None
    tm = _row_tile(m, tm_cap)
    in_specs = [pl.BlockSpec((tm, k), lambda i, j: (i, 0))]
    args = [x]
    if g is not None:
        in_specs.append(pl.BlockSpec((1, k), lambda i, j: (0, 0)))
        args.append(g.reshape(1, k))
    in_specs.append(pl.BlockSpec((k, tn), lambda i, j: (0, j)))
    args.append(w)
    if glu:
        in_specs.append(pl.BlockSpec((k, tn), lambda i, j: (0, j + nj)))
        args.append(w)
    if res is not None:
        in_specs.append(pl.BlockSpec((tm, tn), lambda i, j: (i, j)))
        args.append(res)
    out_specs = [pl.BlockSpec((tm, tn), lambda i, j: (i, j))]
    out_shape = [jax.ShapeDtypeStruct((m, n), F32)]
    scratch = [pltpu.VMEM((tm, k), BF16)]
    ms = 0
    if dual:
        ms = small.shape[0]
        small_cols = pl.BlockSpec((ms, tn), lambda i, j: (0, jnp.where(i == 0, j, nj - 1)))
        in_specs.append(pl.BlockSpec((ms, k), lambda i, j: (0, 0)))
        args.append(small)
        if res is not None:
            in_specs.append(small_cols)
            args.append(small_res)
        out_specs.append(small_cols)
        out_shape.append(jax.ShapeDtypeStruct((ms, n), F32))
        scratch.append(pltpu.VMEM((ms, k), BF16))
    vmem = (2 * (tm + ms) * k * 4 + (tm + ms) * k * 2 + (2 if glu else 1) * 2 * k * tn * 4
            + 4 * (tm + ms) * tn * 4 + 2 * k * tn * 4)
    outs = pl.pallas_call(
        functools.partial(_mm_body, norm=g is not None, mode=mode, dual=dual),
        grid=(m // tm, nj),
        in_specs=in_specs,
        out_specs=out_specs,
        out_shape=out_shape,
        scratch_shapes=scratch,
        compiler_params=_params(vmem, 2),
        name="mm_" + mode,
    )(*args)
    return tuple(outs) if dual else outs[0]


def _ffn_body(*refs, dual):
    if dual:
        x_ref, g_ref, wg_ref, wu_ref, wo_ref, xs_ref, o_ref, os_ref, h_scr, hs_scr = refs
    else:
        x_ref, g_ref, wg_ref, wu_ref, wo_ref, o_ref, h_scr = refs
    first_col = pl.program_id(1) == 0

    def start(src, h_dst, out):
        x = src[...]
        h_dst[...] = _rms(x, g_ref[...]).astype(BF16)
        out[...] = x

    def accumulate(h, out):
        gate = _dot(h, wg_ref[...].astype(BF16))
        up = _dot(h, wu_ref[...].astype(BF16))
        a = (jax.nn.silu(gate) * up).astype(BF16)
        out[...] += _dot(a, wo_ref[...].astype(BF16))

    pl.when(first_col)(lambda: start(x_ref, h_scr, o_ref))
    accumulate(h_scr[...], o_ref)
    if dual:
        @pl.when(pl.program_id(0) == 0)
        def _():
            pl.when(first_col)(lambda: start(xs_ref, hs_scr, os_ref))
            accumulate(hs_scr[...], os_ref)


def _ffn(x, g, w_in, w_out, layer, *, small=None, tf=256, tm_cap=1024):
    m, d = x.shape
    f = w_out.shape[1]
    tm = _row_tile(m, tm_cap)
    nf = f // tf
    dual = small is not None
    ms = small.shape[0] if dual else 0
    in_specs = [
        pl.BlockSpec((tm, d), lambda i, j: (i, 0)),
        pl.BlockSpec((1, d), lambda i, j: (0, 0)),
        pl.BlockSpec((None, d, tf), lambda i, j: (layer, 0, j)),
        pl.BlockSpec((None, d, tf), lambda i, j: (layer, 0, j + nf)),
        pl.BlockSpec((None, tf, d), lambda i, j: (layer, j, 0)),
    ]
    args = [x, g.reshape(1, d), w_in, w_in, w_out]
    out_specs = [pl.BlockSpec((tm, d), lambda i, j: (i, 0))]
    out_shape = [jax.ShapeDtypeStruct((m, d), F32)]
    scratch = [pltpu.VMEM((tm, d), BF16)]
    if dual:
        in_specs.append(pl.BlockSpec((ms, d), lambda i, j: (0, 0)))
        args.append(small)
        out_specs.append(pl.BlockSpec((ms, d), lambda i, j: (0, 0)))
        out_shape.append(jax.ShapeDtypeStruct((ms, d), F32))
        scratch.append(pltpu.VMEM((ms, d), BF16))
    vmem = (4 * (tm + ms) * d * 4 + (tm + ms) * d * 2 + 3 * 2 * d * tf * 4 + 3 * d * tf * 2
            + 3 * (tm + ms) * tf * 4)
    outs = pl.pallas_call(
        functools.partial(_ffn_body, dual=dual),
        grid=(m // tm, nf),
        in_specs=in_specs,
        out_specs=out_specs,
        out_shape=out_shape,
        scratch_shapes=scratch,
        compiler_params=_params(vmem, 2),
        name="ffn",
    )(*args)
    return tuple(outs) if dual else outs[0]


def _xattn_body(x_ref, g_ref, wq_ref, wo_ref, mk_ref, mv_ref, o_ref, wq_s, wo_s, mk_s, mv_s, *, tiles_per_batch):
    @pl.when(pl.program_id(0) == 0)
    def _():
        wq_s[...] = wq_ref[...].astype(BF16)
        wo_s[...] = wo_ref[...].astype(BF16)

    @pl.when(pl.program_id(0) % tiles_per_batch == 0)
    def _():
        for hh in range(X_HEADS):
            mk_s[hh] = mk_ref[:, hh, :].astype(BF16)
            mv_s[hh] = mv_ref[:, hh, :].astype(BF16)

    x = x_ref[...]
    h = _rms(x, g_ref[...]).astype(BF16)
    q = _dot(h, wq_s[...])
    heads = []
    for hh in range(X_HEADS):
        sl = slice(hh * X_HEAD_DIM, (hh + 1) * X_HEAD_DIM)
        s = lax.dot_general(q[:, sl].astype(BF16), mk_s[hh], NT_DIMS, preferred_element_type=F32)
        s = s * X_HEAD_DIM ** -0.5
        e = jnp.exp(s - jnp.max(s, axis=-1, keepdims=True))
        a = e / jnp.sum(e, axis=-1, keepdims=True)
        heads.append(_dot(a.astype(BF16), mv_s[hh]))
    o = jnp.concatenate(heads, axis=1).astype(BF16)
    o_ref[...] = x + _dot(o, wo_s[...])


def _xattn(x, g, w_q, w_o, mem_k, mem_v, layer, rows_per_batch, *, tm_cap=512):
    m, d = x.shape
    xw = w_q.shape[-1]
    n_mem = mem_k.shape[2]
    tm = _row_tile(rows_per_batch, tm_cap)
    tiles_per_batch = rows_per_batch // tm
    vmem = 4 * tm * d * 4 + 2 * 2 * d * xw * 4 + 2 * d * xw * 2 + 8 * n_mem * xw * 4 + 6 * tm * xw * 4
    mem_spec = pl.BlockSpec((None, None, n_mem, X_HEADS, X_HEAD_DIM),
                            lambda i: (layer, i // tiles_per_batch, 0, 0, 0))
    mem_scratch = pltpu.VMEM((X_HEADS, n_mem, X_HEAD_DIM), BF16)
    return pl.pallas_call(
        functools.partial(_xattn_body, tiles_per_batch=tiles_per_batch),
        grid=(m // tm,),
        in_specs=[
            pl.BlockSpec((tm, d), lambda i: (i, 0)),
            pl.BlockSpec((1, d), lambda i: (0, 0)),
            pl.BlockSpec((None, d, xw), lambda i: (layer, 0, 0)),
            pl.BlockSpec((None, xw, d), lambda i: (layer, 0, 0)),
            mem_spec,
            mem_spec,
        ],
        out_specs=pl.BlockSpec((tm, d), lambda i: (i, 0)),
        out_shape=jax.ShapeDtypeStruct((m, d), F32),
        scratch_shapes=[pltpu.VMEM((d, xw), BF16), pltpu.VMEM((xw, d), BF16), mem_scratch, mem_scratch],
        compiler_params=_params(vmem, 1),
        name="xattn",
    )(x, g.reshape(1, d), w_q, w_o, mem_k, mem_v)


def _memkv_body(m_ref, g_ref, wk_ref, wv_ref, k_ref, v_ref):
    mn = _rms(m_ref[...], g_ref[...]).astype(BF16)
    k = _dot(mn, wk_ref[...].astype(BF16))
    v = _dot(mn, wv_ref[...].astype(BF16))
    for hh in range(X_HEADS):
        sl = slice(hh * X_HEAD_DIM, (hh + 1) * X_HEAD_DIM)
        k_ref[:, hh, :] = k[:, sl]
        v_ref[:, hh, :] = v[:, sl]


def _memkv(mem, norm_mem, w_xk, w_xv):
    depth, d, xw = w_xk.shape
    m = mem.shape[0]
    out = jax.ShapeDtypeStruct((depth, m, X_HEADS, X_HEAD_DIM), F32)
    vmem = 2 * m * d * 4 + m * d * 2 + 4 * d * xw * 4 + 2 * d * xw * 2 + 6 * m * xw * 4
    return pl.pallas_call(
        _memkv_body,
        grid=(depth,),
        in_specs=[
            pl.BlockSpec((m, d), lambda l: (0, 0)),
            pl.BlockSpec((None, 1, d), lambda l: (l, 0, 0)),
            pl.BlockSpec((None, d, xw), lambda l: (l, 0, 0)),
            pl.BlockSpec((None, d, xw), lambda l: (l, 0, 0)),
        ],
        out_specs=[pl.BlockSpec((None, m, X_HEADS, X_HEAD_DIM), lambda l: (l, 0, 0, 0))] * 2,
        out_shape=[out, out],
        compiler_params=_params(vmem, 1),
        name="memkv",
    )(mem, norm_mem.reshape(depth, 1, d), w_xk, w_xv)


def _rmsnorm_body(x_ref, g_ref, o_ref):
    o_ref[...] = _rms(x_ref[...], g_ref[...])


def _rmsnorm(x, g, *, tm_cap=512):
    m, d = x.shape
    tm = _row_tile(m, tm_cap)
    return pl.pallas_call(
        _rmsnorm_body,
        grid=(m // tm,),
        in_specs=[pl.BlockSpec((tm, d), lambda i: (i, 0)), pl.BlockSpec((1, d), lambda i: (0, 0))],
        out_specs=pl.BlockSpec((tm, d), lambda i: (i, 0)),
        out_shape=jax.ShapeDtypeStruct((m, d), F32),
        compiler_params=_params(6 * tm * d * 4, 1),
        name="rmsnorm",
    )(x, g.reshape(1, d))


def _suffix_matrix(n):
    r = lax.broadcasted_iota(jnp.int32, (n, n), 0)
    c = lax.broadcasted_iota(jnp.int32, (n, n), 1)
    return jnp.where(r >= c, 1.0, 0.0).astype(BF16)


def _sb_suffix(z, mask, u_mat):
    nz = -z
    lk = jnp.minimum(nz, 0.0) - jnp.log2(1.0 + jnp.exp2(jnp.minimum(z, nz)))
    if mask is not None:
        lk = jnp.where(mask, lk, 0.0)
    hi, lo = _split_hi_lo(lk)
    return _dot(hi, u_mat) + _dot(lo, u_mat)


def _sb_weights(z, incl, mask, acc):
    w = jnp.exp2(z + incl + acc)
    return w if mask is None else jnp.where(mask, w, 0.0)


def _sb_tile(z, mask, acc, u_mat):
    incl = _sb_suffix(z, mask, u_mat)
    return _sb_weights(z, incl, mask, acc), acc + incl[:, 0:1]


def _sbp_body(bias_ref, q_ref, k_ref, v_ref, o_ref, *, tq, tk, scale):
    h = pl.program_id(1)
    i = pl.program_id(2)
    bias = bias_ref[h] * LOG2E
    q = (q_ref[...] * (scale * LOG2E)).astype(BF16)
    u_mat = _suffix_matrix(tk)
    n_diag = tq // tk
    nk = (i + 1) * n_diag

    def group(m, carry, masked):
        acc, out = carry
        starts = [pl.multiple_of((nk - 1 - (n_diag * m + r)) * tk, tk) for r in range(n_diag)]
        zs = [lax.dot_general(q, k_ref[pl.ds(s, tk), :].astype(BF16), NT_DIMS, preferred_element_type=F32) + bias
              for s in starts]
        masks = [None] * n_diag
        if masked:
            qoff = lax.broadcasted_iota(jnp.int32, (tq, tk), 0)
            koff = lax.broadcasted_iota(jnp.int32, (tq, tk), 1)
            masks = [(koff - qoff) < (i * tq - s) for s in starts]
        incls = [_sb_suffix(z, mask, u_mat) for z, mask in zip(zs, masks)]
        for s, z, incl, mask in zip(starts, zs, incls, masks):
            w = _sb_weights(z, incl, mask, acc).astype(BF16)
            acc = acc + incl[:, 0:1]
            out = out + _dot(w, v_ref[pl.ds(s, tk), :].astype(BF16))
        return acc, out

    dh = q_ref.shape[1]
    carry = group(0, (jnp.zeros((tq, 1), F32), jnp.zeros((tq, dh), F32)), True)
    _, out = lax.fori_loop(1, i + 1, lambda m, c: group(m, c, False), carry)
    o_ref[...] = out.astype(o_ref.dtype)


def _sb_attend_prompt(qkv, bias, nb, t, *, tq=512):
    m, d3 = qkv.shape
    d = d3 // 3
    dh = d // SB_HEADS
    tq = min(tq, t)
    tk = min(SB_KEY_TILE, tq)
    nq = t // tq
    vmem = 4 * t * dh * 4 + 4 * tq * dh * 4 + 16 * tq * tk * 4
    return pl.pallas_call(
        functools.partial(_sbp_body, tq=tq, tk=tk, scale=dh ** -0.5),
        grid=(nb, SB_HEADS, nq),
        in_specs=[
            pl.BlockSpec(memory_space=pltpu.SMEM),
            pl.BlockSpec((tq, dh), lambda b, h, i: (b * nq + i, h)),
            pl.BlockSpec((t, dh), lambda b, h, i: (b, SB_HEADS + h)),
            pl.BlockSpec((t, dh), lambda b, h, i: (b, 2 * SB_HEADS + h)),
        ],
        out_specs=pl.BlockSpec((tq, dh), lambda b, h, i: (b * nq + i, h)),
        out_shape=jax.ShapeDtypeStruct((m, d), _mxu_feed_dtype(tq)),
        compiler_params=_params(vmem, 3),
        name="sb_prompt",
    )(bias, qkv, qkv, qkv)


def _sbs_body(pt_ref, bias_ref, qkv_ref, *refs, t, dh, page, npp, scale):
    del pt_ref
    k_refs, v_refs = refs[:npp], refs[npp:2 * npp]
    o_ref, out_s, acc_s = refs[2 * npp:]
    s = pl.program_id(1)
    d = SB_HEADS * dh
    rows = SB_HEADS * t
    u_mat = _suffix_matrix(page)

    def attend(pages, mask, acc, outs):
        qs = [(qkv_ref[:, h * dh:(h + 1) * dh] * (scale * LOG2E)).astype(BF16) for h in range(SB_HEADS)]
        zs = [jnp.concatenate([lax.dot_general(qs[h], get_k(h), NT_DIMS, preferred_element_type=F32)
                               + bias_ref[h] * LOG2E for h in range(SB_HEADS)], axis=0)
              for get_k, _ in pages]
        incls = [_sb_suffix(z, mask, u_mat) for z in zs]
        ws = []
        for z, incl in zip(zs, incls):
            ws.append(_sb_weights(z, incl, mask, acc))
            acc = acc + incl[:, 0:1]
        for w, (_, get_v) in zip(ws, pages):
            outs = [outs[h] + _dot(w[h * t:(h + 1) * t, :].astype(BF16), get_v(h)) for h in range(SB_HEADS)]
        return acc, outs

    def store(acc, outs):
        acc_s[...] = acc
        for h in range(SB_HEADS):
            out_s[h * t:(h + 1) * t, :] = outs[h]

    @pl.when(s == 0)
    def _():
        pad = jnp.zeros((page - t, dh), F32)
        key = lax.broadcasted_iota(jnp.int32, (rows, page), 1)
        qry = lax.broadcasted_iota(jnp.int32, (rows, page), 0) % t
        new_tokens = (
            lambda h: jnp.concatenate([qkv_ref[:, d + h * dh:d + (h + 1) * dh], pad], axis=0).astype(BF16),
            lambda h: jnp.concatenate([qkv_ref[:, 2 * d + h * dh:2 * d + (h + 1) * dh], pad], axis=0).astype(BF16))
        acc, outs = attend([new_tokens], key < qry, jnp.zeros((rows, 1), F32),
                           [jnp.zeros((t, dh), F32)] * SB_HEADS)
        store(acc, outs)

    def cached_page(j):
        return (lambda h: k_refs[j][pl.ds(h, page, stride=SB_HEADS), :].astype(BF16),
                lambda h: v_refs[j][pl.ds(h, page, stride=SB_HEADS), :].astype(BF16))

    acc, outs = attend([cached_page(j) for j in range(npp)], None, acc_s[...],
                       [out_s[h * t:(h + 1) * t, :] for h in range(SB_HEADS)])
    store(acc, outs)

    @pl.when(s == pl.num_programs(1) - 1)
    def _():
        for h in range(SB_HEADS):
            o_ref[:, h * dh:(h + 1) * dh] = outs[h]


def _sb_attend_sample(qkv, bias, cache_k, cache_v, page_table, nb, t, *, npp=8):
    m, d3 = qkv.shape
    d = d3 // 3
    dh = d // SB_HEADS
    n_phys, page = cache_k.shape[0], cache_k.shape[1]
    n_pages = page_table.shape[1]
    while n_pages % npp:
        npp //= 2
    ck = cache_k.reshape(n_phys, page * SB_HEADS, dh)
    cv = cache_v.reshape(n_phys, page * SB_HEADS, dh)

    def cache_spec(j):
        return pl.BlockSpec((None, page * SB_HEADS, dh),
                            lambda b, s, pt: (pt[b, n_pages - 1 - (s * npp + j)], 0, 0))

    vmem = 4 * npp * page * d * 4 + 2 * t * d3 * 4 + 24 * npp * SB_HEADS * t * page * 4
    return pl.pallas_call(
        functools.partial(_sbs_body, t=t, dh=dh, page=page, npp=npp, scale=dh ** -0.5),
        grid_spec=pltpu.PrefetchScalarGridSpec(
            num_scalar_prefetch=1,
            grid=(nb, n_pages // npp),
            in_specs=[pl.BlockSpec(memory_space=pltpu.SMEM), pl.BlockSpec((t, d3), lambda b, s, pt: (b, 0))]
            + [cache_spec(j) for j in range(npp)] * 2,
            out_specs=pl.BlockSpec((t, d), lambda b, s, pt: (b, 0)),
            scratch_shapes=[pltpu.VMEM((SB_HEADS * t, dh), F32), pltpu.VMEM((SB_HEADS * t, 1), F32)],
        ),
        out_shape=jax.ShapeDtypeStruct((m, d), F32),
        compiler_params=_params(vmem, 2),
        name="sb_sample",
    )(page_table, bias, qkv, *([ck] * npp), *([cv] * npp))


def _cmul(ar, ai, br, bi):
    return ar * br - ai * bi, ar * bi + ai * br


def _gelu_tanh(x):
    return 0.5 * x * (1.0 + jnp.tanh(math.sqrt(2.0 / math.pi) * (x + 0.044715 * (x * x * x))))


def _s5_body(h_ref, ar_ref, ai_ref, br_ref, bi_ref, cr_ref, ci_ref, d_ref, s0r_ref, s0i_ref,
             y_ref, sr_ref, si_ref, xr_s, xi_s, pr_s, pi_s, cr_s, ci_s, *, nbk):
    t = pl.program_id(2)
    rows8, n_state = pr_s.shape
    row = lax.broadcasted_iota(jnp.int32, (rows8, n_state), 0)

    def last_step_to_all_rows(x):
        x = jnp.where(row >= rows8 - nbk, x, 0.0)
        have = nbk
        while have < rows8:
            x = x + pltpu.roll(x, rows8 - have, 0)
            have *= 2
        return x

    @pl.when(t == 0)
    def _():
        ar, ai = ar_ref[...], ai_ref[...]
        pr, pi = ar, ai
        for r in range(rows8):
            if r and r % nbk == 0:
                pr, pi = _cmul(pr, pi, ar, ai)
            pr_s[r:r + 1, :] = pr
            pi_s[r:r + 1, :] = pi
        cr_s[...] = jnp.zeros_like(cr_s)
        ci_s[...] = jnp.zeros_like(ci_s)
        cr_s[rows8 - nbk:, :] = s0r_ref[...]
        ci_s[rows8 - nbk:, :] = s0i_ref[...]
        cr_s[...] = last_step_to_all_rows(cr_s[...])
        ci_s[...] = last_step_to_all_rows(ci_s[...])

    tt = h_ref.shape[1]
    n_slab = xr_s.shape[0]
    lanes = [slice(s * V7X_LANES, (s + 1) * V7X_LANES) for s in range(n_slab)]

    def batch_rows(b):
        return pl.ds(b, tt, stride=nbk) if nbk > 1 else pl.ds(0, tt)

    b_r, b_i = br_ref[...].astype(BF16), bi_ref[...].astype(BF16)
    for b in range(nbk):
        u = h_ref[b].astype(BF16)
        bu_r, bu_i = _dot(u, b_r), _dot(u, b_i)
        for s in range(n_slab):
            xr_s[s, batch_rows(b), :] = bu_r[:, lanes[s]]
            xi_s[s, batch_rows(b), :] = bu_i[:, lanes[s]]
    pw_r, pw_i = pr_s[...], pi_s[...]

    def tile(blk, carry):
        c_r, c_i = carry
        rows = pl.ds(pl.multiple_of(blk * rows8, rows8), rows8)
        x_r = jnp.concatenate([xr_s[s, rows, :] for s in range(n_slab)], axis=1)
        x_i = jnp.concatenate([xi_s[s, rows, :] for s in range(n_slab)], axis=1)
        dist = 1
        while dist * nbk < rows8:
            lo = (dist - 1) * nbk
            a_r, a_i = pw_r[lo:lo + 1, :], pw_i[lo:lo + 1, :]
            keep = row >= dist * nbk
            sh_r = jnp.where(keep, pltpu.roll(x_r, dist * nbk, 0), 0.0)
            sh_i = jnp.where(keep, pltpu.roll(x_i, dist * nbk, 0), 0.0)
            d_r, d_i = _cmul(a_r, a_i, sh_r, sh_i)
            x_r, x_i = x_r + d_r, x_i + d_i
            dist *= 2
        d_r, d_i = _cmul(pw_r, pw_i, c_r, c_i)
        x_r, x_i = x_r + d_r, x_i + d_i
        for s in range(n_slab):
            xr_s[s, rows, :] = x_r[:, lanes[s]]
            xi_s[s, rows, :] = x_i[:, lanes[s]]
        return last_step_to_all_rows(x_r), last_step_to_all_rows(x_i)

    c_r, c_i = lax.fori_loop(0, xr_s.shape[1] // rows8, tile, (cr_s[...], ci_s[...]))
    cr_s[...] = c_r
    ci_s[...] = c_i

    c_re, c_im = cr_ref[...].astype(BF16), ci_ref[...].astype(BF16)
    for b in range(nbk):
        st_r = jnp.concatenate([xr_s[s, batch_rows(b), :] for s in range(n_slab)], axis=1).astype(BF16)
        st_i = jnp.concatenate([xi_s[s, batch_rows(b), :] for s in range(n_slab)], axis=1).astype(BF16)
        y = _dot(st_r, c_re) - _dot(st_i, c_im)
        y_ref[b] = _gelu_tanh(y + d_ref[...] * h_ref[b]).astype(y_ref.dtype)

    @pl.when(t == pl.num_programs(2) - 1)
    def _():
        sr_ref[...] = c_r[rows8 - nbk:, :]
        si_ref[...] = c_i[rows8 - nbk:, :]


def _s5_discretize(a_re, a_im, log_dt, b_re, b_im, c_re, c_im):
    g, p = a_re.shape
    gb = S5_CH_BLOCK // S5_GROUP_CH
    nblk = g // gb
    dt = jnp.exp(log_dt)[:, None]
    mag = jnp.exp(a_re * dt)
    abar_r, abar_i = mag * jnp.cos(a_im * dt), mag * jnp.sin(a_im * dt)
    xr, xi = abar_r - 1.0, abar_i
    den = a_re * a_re + a_im * a_im
    coef_r = (xr * a_re + xi * a_im) / den
    coef_i = (xi * a_re - xr * a_im) / den
    bbar_r = coef_r[..., None] * b_re - coef_i[..., None] * b_im
    bbar_i = coef_r[..., None] * b_im + coef_i[..., None] * b_re
    eye = jnp.eye(gb, dtype=F32)

    def b_big(bb):
        blk = bb.reshape(nblk, gb, p, S5_GROUP_CH).transpose(0, 1, 3, 2)
        return blk[:, :, :, None, :] * eye[None, :, None, :, None]

    def c_big(cc):
        blk = cc.reshape(nblk, gb, S5_GROUP_CH, p).transpose(0, 1, 3, 2)
        return blk[:, :, :, None, :] * eye[None, :, None, :, None]

    n_state = gb * p
    return (abar_r.reshape(1, g * p), abar_i.reshape(1, g * p),
            b_big(bbar_r).reshape(nblk, S5_CH_BLOCK, n_state), b_big(bbar_i).reshape(nblk, S5_CH_BLOCK, n_state),
            c_big(c_re).reshape(nblk, n_state, S5_CH_BLOCK), c_big(c_im).reshape(nblk, n_state, S5_CH_BLOCK))


def _s5_core(h, disc, d_skip, s0r, s0i, nb, t, *, tt_cap=256):
    m, d = h.shape
    nbk = math.gcd(nb, V7X_SUBLANES)
    abar_r, abar_i, bbr, bbi, ccr, cci = disc
    nblk, cb, n_state = bbr.shape
    tt = _row_tile(t, tt_cap)
    nt = t // tt
    ng = nb // nbk
    state = jax.ShapeDtypeStruct((ng, nbk, nblk * n_state), F32)
    lane_spec = pl.BlockSpec((1, n_state), lambda b, c, i: (0, c))
    state_spec = pl.BlockSpec((None, nbk, n_state), lambda b, c, i: (b, 0, c))
    b_spec = pl.BlockSpec((None, cb, n_state), lambda b, c, i: (c, 0, 0))
    c_spec = pl.BlockSpec((None, n_state, cb), lambda b, c, i: (c, 0, 0))
    row_spec = pl.BlockSpec((nbk, tt, cb), lambda b, c, i: (b, i, c))
    vmem = 4 * tt * nbk * cb * 4 + 16 * cb * n_state * 4 + 8 * tt * nbk * n_state * 4
    lane_tile = pltpu.VMEM((V7X_SUBLANES, n_state), F32)
    slabs = pltpu.VMEM((n_state // V7X_LANES, tt * nbk, V7X_LANES), F32)
    y, sr, si = pl.pallas_call(
        functools.partial(_s5_body, nbk=nbk),
        grid=(ng, nblk, nt),
        in_specs=[row_spec, lane_spec, lane_spec, b_spec, b_spec, c_spec, c_spec,
                  pl.BlockSpec((1, cb), lambda b, c, i: (0, c)), state_spec, state_spec],
        out_specs=[row_spec, state_spec, state_spec],
        out_shape=[jax.ShapeDtypeStruct((nb, t, d), _mxu_feed_dtype(tt)), state, state],
        scratch_shapes=[slabs, slabs, lane_tile, lane_tile, lane_tile, lane_tile],
        compiler_params=_params(vmem, 3),
        name="s5",
    )(h.reshape(nb, t, d), abar_r, abar_i, bbr, bbi, ccr, cci, d_skip.reshape(1, d),
      s0r.reshape(ng, nbk, -1), s0i.reshape(ng, nbk, -1))
    return y.reshape(m, d), sr.reshape(nb, -1), si.reshape(nb, -1)


def _conv_body(b_ref, c_ref, v_ref, w_ref, p_ref, y_ref, s_ref):
    z = c_ref[...] * v_ref[...]
    t = z.shape[0]
    row = lax.broadcasted_iota(jnp.int32, z.shape, 0)
    p0, p1 = p_ref[0:1, :], p_ref[1:2, :]
    z1 = jnp.where(row >= 1, pltpu.roll(z, 1, 0), p1)
    z2 = jnp.where(row >= 2, pltpu.roll(z, 2, 0), jnp.where(row == 0, p0, p1))
    y = w_ref[0:1, :] * z2 + w_ref[1:2, :] * z1 + w_ref[2:3, :] * z
    y_ref[...] = (b_ref[...] * y).astype(y_ref.dtype)
    s_ref[...] = z[t - (CONV_W - 1):, :]


def _conv_core(bcv, w_conv, prev, nb, t, *, tc=512):
    m, d3 = bcv.shape
    d = d3 // 3
    nc = d // tc
    vmem = 12 * t * tc * 4
    return pl.pallas_call(
        _conv_body,
        grid=(nb, nc),
        in_specs=[
            pl.BlockSpec((t, tc), lambda b, j: (b, j)),
            pl.BlockSpec((t, tc), lambda b, j: (b, nc + j)),
            pl.BlockSpec((t, tc), lambda b, j: (b, 2 * nc + j)),
            pl.BlockSpec((CONV_W, tc), lambda b, j: (0, j)),
            pl.BlockSpec((None, CONV_W - 1, tc), lambda b, j: (b, 0, j)),
        ],
        out_specs=[pl.BlockSpec((t, tc), lambda b, j: (b, j)),
                   pl.BlockSpec((None, CONV_W - 1, tc), lambda b, j: (b, 0, j))],
        out_shape=[jax.ShapeDtypeStruct((m, d), _mxu_feed_dtype(t)), jax.ShapeDtypeStruct((nb, CONV_W - 1, d), F32)],
        compiler_params=_params(vmem, 2),
        name="conv",
    )(bcv, bcv, bcv, w_conv, prev)


def _gla_gate_body(x_ref, g_ref, w1_ref, w2_ref, b_ref, o_ref):
    h = _rms(x_ref[...], g_ref[...]).astype(BF16)
    low = _dot(h, w1_ref[...].astype(BF16))
    y = _dot(low.astype(BF16), w2_ref[...].astype(BF16)) + b_ref[...]
    o_ref[...] = -_softplus(-y) / GLA_TAU


def _gla_gate(x, g, w1, w2, b, *, tm_cap=512):
    m, d = x.shape
    r = w1.shape[1]
    gk = w2.shape[1]
    rp = V7X_LANES
    w1p = jnp.pad(w1, ((0, 0), (0, rp - r)))
    w2p = jnp.pad(w2, ((0, rp - r), (0, 0)))
    tm = _row_tile(m, tm_cap)
    vmem = 3 * tm * d * 4 + 4 * d * rp * 4 + 4 * rp * gk * 4 + 6 * tm * gk * 4
    return pl.pallas_call(
        _gla_gate_body,
        grid=(m // tm,),
        in_specs=[
            pl.BlockSpec((tm, d), lambda i: (i, 0)),
            pl.BlockSpec((1, d), lambda i: (0, 0)),
            pl.BlockSpec((d, rp), lambda i: (0, 0)),
            pl.BlockSpec((rp, gk), lambda i: (0, 0)),
            pl.BlockSpec((1, gk), lambda i: (0, 0)),
        ],
        out_specs=pl.BlockSpec((tm, gk), lambda i: (i, 0)),
        out_shape=jax.ShapeDtypeStruct((m, gk), F32),
        compiler_params=_params(vmem, 1),
        name="gla_gate",
    )(x, g.reshape(1, d), w1p, w2p, b.reshape(1, gk))


def _cumsum_rows(x):
    n = x.shape[0]
    row = lax.broadcasted_iota(jnp.int32, x.shape, 0)
    dist = 1
    while dist < n:
        x = x + jnp.where(row >= dist, pltpu.roll(x, dist, 0), 0.0)
        dist *= 2
    return x


def _gla_chunk(q, k, v, glog, st):
    c, dk = q.shape
    b = _cumsum_rows(glog)
    o = lax.dot_general((q * jnp.exp(b)).astype(BF16), st.astype(BF16), NT_DIMS, preferred_element_type=F32)
    rowc = lax.broadcasted_iota(jnp.int32, (c, 1), 0)
    lane = lax.broadcasted_iota(jnp.int32, (1, c), 1)
    rsub = lax.broadcasted_iota(jnp.int32, (GLA_SUB, 1), 0)
    att_rows = []
    for blk in range(c // GLA_SUB):
        lo = blk * GLA_SUB
        b_i = b[lo:lo + GLA_SUB]
        q_i = q[lo:lo + GLA_SUB]
        if blk > 0:
            b_0 = b[lo - 1:lo]
            q_off = (q_i * jnp.exp(b_i - b_0)).astype(BF16)
            k_off = (k * jnp.exp(jnp.where(rowc < lo, b_0 - b, NEG_INF))).astype(BF16)
            att = lax.dot_general(q_off, k_off, NT_DIMS, preferred_element_type=F32)
        else:
            att = jnp.zeros((GLA_SUB, c), F32)
        for jj in range(GLA_SUB):
            j = lo + jj
            dec = jnp.exp(jnp.where(rsub >= jj, b_i - b[j:j + 1], NEG_INF))
            col = jnp.sum(q_i * k[j:j + 1] * dec, axis=1, keepdims=True)
            att = att + jnp.where(lane == j, col, 0.0)
        att_rows.append(att)
    att = att_rows[0] if len(att_rows) == 1 else jnp.concatenate(att_rows, axis=0)
    vb = v.astype(BF16)
    o = o + _dot(att.astype(BF16), vb)
    b_last = b[c - 1:c]
    k_dec = (k * jnp.exp(b_last - b)).astype(BF16)
    st = st * jnp.exp(b_last) + lax.dot_general(vb, k_dec, TN_DIMS, preferred_element_type=F32)
    return o, st


def _gla_body(*refs, chunk, has_prev, scale):
    refs = list(refs)
    q_ref, k_ref, v_ref, r_ref, gl_ref, ng_ref = refs[:6]
    s0_ref = refs[6] if has_prev else None
    o_ref, s_ref, st_s = refs[-3:]
    t = pl.program_id(1)
    n_heads, dv, dk = st_s.shape

    @pl.when(t == 0)
    def _():
        for h in range(n_heads):
            st_s[h] = s0_ref[h].T if has_prev else jnp.zeros((dv, dk), F32)

    def step(ci, carry):
        rows = pl.ds(pl.multiple_of(ci * chunk, chunk), chunk)
        for h in range(n_heads):
            ck, cv = slice(h * dk, (h + 1) * dk), slice(h * dv, (h + 1) * dv)
            o, st = _gla_chunk(q_ref[rows, ck] * scale, k_ref[rows, ck], v_ref[rows, cv], gl_ref[rows, ck], st_s[h])
            st_s[h] = st
            o = o * lax.rsqrt(jnp.mean(o * o, axis=-1, keepdims=True) + EPS) * ng_ref[...]
            o_ref[rows, cv] = (o * jax.nn.silu(r_ref[rows, cv])).astype(o_ref.dtype)
        return carry

    lax.fori_loop(0, q_ref.shape[0] // chunk, step, 0)

    @pl.when(t == pl.num_programs(1) - 1)
    def _():
        for h in range(n_heads):
            s_ref[h] = st_s[h].T


def _gla_core(qkvr, glog, norm_g, prev, nb, t, *, tt_cap=256):
    m = qkvr.shape[0]
    gk = glog.shape[1]
    dk = gk // GLA_HEADS
    d = (qkvr.shape[1] - 2 * gk) // 2
    dv = d // GLA_HEADS
    chunk = GLA_CHUNK if t % GLA_CHUNK == 0 else t
    tt = _row_tile(t, tt_cap)
    nt = t // tt
    in_specs = [
        pl.BlockSpec((tt, gk), lambda b, i: (b * nt + i, 0)),
        pl.BlockSpec((tt, gk), lambda b, i: (b * nt + i, 1)),
        pl.BlockSpec((tt, d), lambda b, i: (b * nt + i, 2 * gk // d)),
        pl.BlockSpec((tt, d), lambda b, i: (b * nt + i, 2 * gk // d + 1)),
        pl.BlockSpec((tt, gk), lambda b, i: (b * nt + i, 0)),
        pl.BlockSpec((1, dv), lambda b, i: (0, 0)),
    ]
    args = [qkvr, qkvr, qkvr, qkvr, glog, norm_g.reshape(1, dv)]
    state_spec = pl.BlockSpec((None, GLA_HEADS, dk, dv), lambda b, i: (b, 0, 0, 0))
    if prev is not None:
        in_specs.append(state_spec)
        args.append(prev)
    vmem = 4 * tt * (3 * gk + 3 * d) * 4 + 7 * GLA_HEADS * dk * dv * 4 + 64 * GLA_HEADS * chunk * dv * 4
    return pl.pallas_call(
        functools.partial(_gla_body, chunk=chunk, has_prev=prev is not None, scale=dk ** -0.5),
        grid=(nb, nt),
        in_specs=in_specs,
        out_specs=[pl.BlockSpec((tt, d), lambda b, i: (b * nt + i, 0)), state_spec],
        out_shape=[jax.ShapeDtypeStruct((m, d), _mxu_feed_dtype(chunk)),
                   jax.ShapeDtypeStruct((nb, GLA_HEADS, dk, dv), F32)],
        scratch_shapes=[pltpu.VMEM((GLA_HEADS, dv, dk), F32)],
        compiler_params=_params(vmem, 2),
        name="gla",
    )(*args)


def _trunk(xp, xs, prompt_dims, sample_dims, mem_p, mem_s, p, s5_disc, sample):
    (nbp, tp), (nbs, ts) = prompt_dims, sample_dims
    d = xp.shape[1]
    depth = p["norm_mix"].shape[0]
    new_p, new_s = {}, {}
    for i in range(depth):
        g_mix = p["norm_mix"][i]
        kind = i % 4
        if kind == 0:
            qkv_p, qkv_s = _mm(xp, p["w_sb_qkv"], g=g_mix, small=xs)
            o_p = _sb_attend_prompt(qkv_p, p["sb_bias"], nbp, tp)
            o_s = _sb_attend_sample(qkv_s, p["sb_bias"], sample["cache_sb_k"], sample["cache_sb_v"],
                                    sample["page_table"], nbs, ts)
            xp, xs = _mm(o_p, p["w_sb_out"], res=xp, small=o_s, small_res=xs)
            for new, qkv, nb, t in ((new_p, qkv_p, nbp, tp), (new_s, qkv_s, nbs, ts)):
                shp = (nb, t, SB_HEADS, d // SB_HEADS)
                new["sb"] = (qkv[:, d:2 * d].reshape(shp), qkv[:, 2 * d:].reshape(shp))
        elif kind == 1:
            n_state = d // S5_GROUP_CH * S5_STATE
            zeros = jnp.zeros((nbp, n_state), F32)
            y_p, sr_p, si_p = _s5_core(_rmsnorm(xp, g_mix), s5_disc, p["s5_d"], zeros, zeros, nbp, tp)
            y_s, sr_s, si_s = _s5_core(_rmsnorm(xs, g_mix), s5_disc, p["s5_d"], sample["state_s5_re"],
                                       sample["state_s5_im"], nbs, ts)
            xp, xs = _mm(y_p, p["w_s5_glu"], res=xp, glu=True, small=y_s, small_res=xs)
            for new, sr, si, nb in ((new_p, sr_p, si_p, nbp), (new_s, sr_s, si_s, nbs)):
                sshape = (nb, d // S5_GROUP_CH, S5_STATE)
                new["s5"] = (sr.reshape(sshape), si.reshape(sshape))
        elif kind == 2:
            bcv_p, bcv_s = _mm(xp, p["w_conv_in"], g=g_mix, small=xs)
            yg_p, new_p["conv"] = _conv_core(bcv_p, p["w_conv"], jnp.zeros((nbp, CONV_W - 1, d), F32), nbp, tp)
            yg_s, new_s["conv"] = _conv_core(bcv_s, p["w_conv"], sample["state_conv"], nbs, ts)
            xp, xs = _mm(yg_p, p["w_conv_out"], res=xp, small=yg_s, small_res=xs)
        else:
            qkvr_p, qkvr_s = _mm(xp, p["w_gla_in"], g=g_mix, small=xs)
            gate = (g_mix, p["w_gla_g1"], p["w_gla_g2"], p["b_gla_g"])
            o_p, new_p["gla"] = _gla_core(qkvr_p, _gla_gate(xp, *gate), p["gla_norm"], None, nbp, tp)
            o_s, new_s["gla"] = _gla_core(qkvr_s, _gla_gate(xs, *gate), p["gla_norm"], sample["state_gla"], nbs, ts)
            xp, xs = _mm(o_p, p["w_gla_out"], res=xp, small=o_s, small_res=xs)
        xp = _xattn(xp, p["norm_xattn"][i], p["w_xq"], p["w_xo"], mem_p[0], mem_p[1], i, tp)
        xs = _xattn(xs, p["norm_xattn"][i], p["w_xq"], p["w_xo"], mem_s[0], mem_s[1], i, ts)
        xp, xs = _ffn(xp, p["norm_ffn"][i], p["w_ffn_in"], p["w_ffn_out"], i, small=xs)
    y_p = _rmsnorm(xp, p["norm_final"]).reshape(nbp, tp, d)
    y_s = _rmsnorm(xs, p["norm_final"]).reshape(nbs, ts, d)
    return y_p, y_s, new_p, new_s


def kernel(x_prompt, x_sample, mem_prompt, cache_sb_k, cache_sb_v, page_table, state_s5_re, state_s5_im, state_conv, state_gla, cache_mem_k, cache_mem_v, norm_mix, norm_xattn, norm_mem, norm_ffn, norm_final, w_sb_qkv, w_sb_out, sb_bias, s5_a_re, s5_a_im, s5_log_dt, s5_b_re, s5_b_im, s5_c_re, s5_c_im, s5_d, w_s5_glu, w_conv_in, w_conv, w_conv_out, w_gla_in, w_gla_g1, w_gla_g2, b_gla_g, gla_norm, w_gla_out, w_xq, w_xk, w_xv, w_xo, w_ffn_in, w_ffn_out):
    p = dict(norm_mix=norm_mix, norm_xattn=norm_xattn, norm_ffn=norm_ffn, norm_final=norm_final,
             w_sb_qkv=w_sb_qkv, w_sb_out=w_sb_out, sb_bias=sb_bias, s5_d=s5_d, w_s5_glu=w_s5_glu,
             w_conv_in=w_conv_in, w_conv=w_conv, w_conv_out=w_conv_out,
             w_gla_in=w_gla_in, w_gla_g1=w_gla_g1, w_gla_g2=w_gla_g2, b_gla_g=b_gla_g,
             gla_norm=gla_norm, w_gla_out=w_gla_out, w_xq=w_xq, w_xo=w_xo,
             w_ffn_in=w_ffn_in, w_ffn_out=w_ffn_out)
    nbp, tp, d = x_prompt.shape
    nbs, ts, _ = x_sample.shape
    depth = norm_mix.shape[0]
    n_mem = mem_prompt.shape[1]
    mshape = (depth, nbp, n_mem, X_HEADS, X_HEAD_DIM)
    mem_k_p, mem_v_p = _memkv(mem_prompt.reshape(nbp * n_mem, d), norm_mem, w_xk, w_xv)
    mem_k_p, mem_v_p = mem_k_p.reshape(mshape), mem_v_p.reshape(mshape)
    s5_disc = _s5_discretize(s5_a_re, s5_a_im, s5_log_dt, s5_b_re, s5_b_im, s5_c_re, s5_c_im)

    sample = dict(cache_sb_k=cache_sb_k, cache_sb_v=cache_sb_v, page_table=page_table,
                  state_s5_re=state_s5_re.reshape(nbs, -1), state_s5_im=state_s5_im.reshape(nbs, -1),
                  state_conv=state_conv, state_gla=state_gla)
    y_p, y_s, new_p, new_s = _trunk(x_prompt.reshape(nbp * tp, d), x_sample.reshape(nbs * ts, d),
                                    (nbp, tp), (nbs, ts), (mem_k_p, mem_v_p), (cache_mem_k, cache_mem_v),
                                    p, s5_disc, sample)

    return (y_p, y_s, new_p["sb"][0], new_p["sb"][1], new_s["sb"][0], new_s["sb"][1],
            new_p["s5"][0], new_p["s5"][1], new_s["s5"][0], new_s["s5"][1],
            new_p["conv"], new_s["conv"], new_p["gla"], new_s["gla"],
            mem_k_p, mem_v_p)
```

```python
import functools
import math

import jax
import jax.numpy as jnp
from jax import lax
from jax.experimental import pallas as pl
from jax.experimental.pallas import tpu as pltpu

F32 = jnp.float32
BF16 = jnp.bfloat16
EPS = 1e-6
NEG_INF = float("-inf")
LOG2E = 1.4426950408889634

V7X_VMEM_BYTES = 64 * 1024 * 1024
V7X_SUBLANES = 8
V7X_LANES = 128

SB_HEADS = 16
SB_KEY_TILE = 256
S5_GROUP_CH = 16
S5_STATE = 64
S5_CH_BLOCK = 256
CONV_W = 3
GLA_HEADS = 4
GLA_TAU = 16.0
GLA_CHUNK = 32
GLA_SUB = 8
X_HEADS = 4
X_HEAD_DIM = 128

NT_DIMS = (((1,), (1,)), ((), ()))
TN_DIMS = (((0,), (0,)), ((), ()))


def _params(vmem_bytes, n_grid):
    limit = int(min(V7X_VMEM_BYTES - (6 << 20), max(vmem_bytes * 5 // 4 + (4 << 20), 16 << 20)))
    return pltpu.CompilerParams(dimension_semantics=("arbitrary",) * n_grid, vmem_limit_bytes=limit)


def _rms(x, g):
    return x * lax.rsqrt(jnp.mean(x * x, axis=-1, keepdims=True) + EPS) * g


def _softplus(z):
    return jnp.maximum(z, 0.0) + jnp.log1p(jnp.exp(-jnp.abs(z)))


def _dot(a, b):
    return jnp.dot(a, b, preferred_element_type=F32)


def _split_hi_lo(x):
    hi = x.astype(BF16)
    lo = (x - hi.astype(F32)).astype(BF16)
    return hi, lo


def _mxu_feed_dtype(rows_per_store):
    return BF16 if rows_per_store % (2 * V7X_SUBLANES) == 0 else F32


def _row_tile(m, cap):
    t = min(m, cap)
    while m % t:
        t //= 2
    return t


def _mm_body(*refs, norm, mode, dual):
    refs = list(refs)
    x_ref = refs.pop(0)
    g_ref = refs.pop(0) if norm else None
    w_ref = refs.pop(0)
    w2_ref = refs.pop(0) if mode == "glu" else None
    has_res = mode in ("res", "glu")
    res_ref = refs.pop(0) if has_res else None
    xs_ref = refs.pop(0) if dual else None
    ress_ref = refs.pop(0) if dual and has_res else None
    o_ref = refs.pop(0)
    os_ref = refs.pop(0) if dual else None
    lhs = refs.pop(0)
    tm = x_ref.shape[0]
    first_col = pl.program_id(1) == 0
    first_tile = pl.program_id(0) == 0

    def load_lhs(src, rows):
        x = src[...]
        if norm:
            x = _rms(x, g_ref[...])
        lhs[rows, :] = x.astype(BF16)

    def emit(n_rows):
        a = lhs[0:n_rows, :]
        y = _dot(a, w_ref[...].astype(BF16))
        if mode == "glu":
            y = y * jax.nn.sigmoid(_dot(a, w2_ref[...].astype(BF16)))
        o_ref[...] = res_ref[...] + y[0:tm] if has_res else y[0:tm]
        if n_rows > tm:
            os_ref[...] = ress_ref[...] + y[tm:n_rows] if has_res else y[tm:n_rows]

    pl.when(first_col)(lambda: load_lhs(x_ref, slice(0, tm)))
    if dual:
        n_all = lhs.shape[0]
        pl.when(jnp.logical_and(first_col, first_tile))(lambda: load_lhs(xs_ref, slice(tm, n_all)))
        pl.when(first_tile)(lambda: emit(n_all))
        pl.when(jnp.logical_not(first_tile))(lambda: emit(tm))
    else:
        emit(tm)


def _mm(x, w, *, g=None, res=None, glu=False, small=None, small_res=None, tm_cap=1024):
    m, k = x.shape
    n = w.shape[1] // (2 if glu else 1)
    mode = "glu" if glu else ("res" if res is not None else "plain")
    tn = 1024 if (mode == "plain" and n % 1024 == 0) else 512
    nj = n // tn
    dual = small is not None
    tm = _row_tile(m, tm_cap)
    in_specs = [pl.BlockSpec((tm, k), lambda i, j: (i, 0))]
    args = [x]
    if g is not None:
        in_specs.append(pl.BlockSpec((1, k), lambda i, j: (0, 0)))
        args.append(g.reshape(1, k))
    in_specs.append(pl.BlockSpec((k, tn), lambda i, j: (0, j)))
    args.append(w)
    if glu:
        in_specs.append(pl.BlockSpec((k, tn), lambda i, j: (0, j + nj)))
        args.append(w)
    if res is not None:
        in_specs.append(pl.BlockSpec((tm, tn), lambda i, j: (i, j)))
        args.append(res)
    out_specs = [pl.BlockSpec((tm, tn), lambda i, j: (i, j))]
    out_shape = [jax.ShapeDtypeStruct((m, n), F32)]
    ms = 0
    if dual:
        ms = small.shape[0]
        small_cols = pl.BlockSpec((ms, tn), lambda i, j: (0, jnp.where(i == 0, j, nj - 1)))
        in_specs.append(pl.BlockSpec((ms, k), lambda i, j: (0, 0)))
        args.append(small)
        if res is not None:
            in_specs.append(small_cols)
            args.append(small_res)
        out_specs.append(small_cols)
        out_shape.append(jax.ShapeDtypeStruct((ms, n), F32))
    vmem = (2 * (tm + ms) * k * 4 + (tm + ms) * k * 2 + (2 if glu else 1) * 2 * k * tn * 4
            + 4 * (tm + ms) * tn * 4 + 2 * k * tn * 4)
    outs = pl.pallas_call(
        functools.partial(_mm_body, norm=g is not None, mode=mode, dual=dual),
        grid=(m // tm, nj),
        in_specs=in_specs,
        out_specs=out_specs,
        out_shape=out_shape,
        scratch_shapes=[pltpu.VMEM((tm + ms, k), BF16)],
        compiler_params=_params(vmem, 2),
        name="mm_" + mode,
    )(*args)
    return tuple(outs) if dual else outs[0]


def _ffn_body(*refs, dual, final_norm):
    refs = list(refs)
    x_ref, g_ref, wg_ref, wu_ref, wo_ref = refs[:5]
    refs = refs[5:]
    gf_ref = refs.pop(0) if final_norm else None
    xs_ref = refs.pop(0) if dual else None
    o_ref = refs.pop(0)
    os_ref = refs.pop(0) if dual else None
    h_scr = refs.pop(0)
    tm = x_ref.shape[0]
    first_col = pl.program_id(1) == 0
    last_col = pl.program_id(1) == pl.num_programs(1) - 1
    first_tile = pl.program_id(0) == 0

    def start(src, rows, out):
        x = src[...]
        h_scr[rows, :] = _rms(x, g_ref[...]).astype(BF16)
        out[...] = x

    def accumulate(n_rows):
        h = h_scr[0:n_rows, :]
        gate = _dot(h, wg_ref[...].astype(BF16))
        up = _dot(h, wu_ref[...].astype(BF16))
        a = (jax.nn.silu(gate) * up).astype(BF16)
        y = _dot(a, wo_ref[...].astype(BF16))
        o_ref[...] += y[0:tm]
        if n_rows > tm:
            os_ref[...] += y[tm:n_rows]

    pl.when(first_col)(lambda: start(x_ref, slice(0, tm), o_ref))
    if dual:
        n_all = h_scr.shape[0]
        pl.when(jnp.logical_and(first_col, first_tile))(lambda: start(xs_ref, slice(tm, n_all), os_ref))
        pl.when(first_tile)(lambda: accumulate(n_all))
        pl.when(jnp.logical_not(first_tile))(lambda: accumulate(tm))
    else:
        accumulate(tm)
    if final_norm:
        @pl.when(last_col)
        def _():
            o_ref[...] = _rms(o_ref[...], gf_ref[...])

        if dual:
            @pl.when(jnp.logical_and(last_col, first_tile))
            def _():
                os_ref[...] = _rms(os_ref[...], gf_ref[...])


def _ffn(x, g, w_in, w_out, layer, *, small=None, final_g=None, tf=256, tm_cap=1024):
    m, d = x.shape
    f = w_out.shape[1]
    tm = _row_tile(m, tm_cap)
    nf = f // tf
    dual = small is not None
    ms = small.shape[0] if dual else 0
    in_specs = [
        pl.BlockSpec((tm, d), lambda i, j: (i, 0)),
        pl.BlockSpec((1, d), lambda i, j: (0, 0)),
        pl.BlockSpec((None, d, tf), lambda i, j: (layer, 0, j)),
        pl.BlockSpec((None, d, tf), lambda i, j: (layer, 0, j + nf)),
        pl.BlockSpec((None, tf, d), lambda i, j: (layer, j, 0)),
    ]
    args = [x, g.reshape(1, d), w_in, w_in, w_out]
    if final_g is not None:
        in_specs.append(pl.BlockSpec((1, d), lambda i, j: (0, 0)))
        args.append(final_g.reshape(1, d))
    out_specs = [pl.BlockSpec((tm, d), lambda i, j: (i, 0))]
    out_shape = [jax.ShapeDtypeStruct((m, d), F32)]
    if dual:
        in_specs.append(pl.BlockSpec((ms, d), lambda i, j: (0, 0)))
        args.append(small)
        out_specs.append(pl.BlockSpec((ms, d), lambda i, j: (0, 0)))
        out_shape.append(jax.ShapeDtypeStruct((ms, d), F32))
    vmem = (4 * (tm + ms) * d * 4 + (tm + ms) * d * 2 + 3 * 2 * d * tf * 4 + 3 * d * tf * 2
            + 3 * (tm + ms) * tf * 4)
    outs = pl.pallas_call(
        functools.partial(_ffn_body, dual=dual, final_norm=final_g is not None),
        grid=(m // tm, nf),
        in_specs=in_specs,
        out_specs=out_specs,
        out_shape=out_shape,
        scratch_shapes=[pltpu.VMEM((tm + ms, d), BF16)],
        compiler_params=_params(vmem, 2),
        name="ffn",
    )(*args)
    return tuple(outs) if dual else outs[0]


def _xattn_body(x_ref, g_ref, wq_ref, wo_ref, mk_ref, mv_ref, o_ref, wq_s, wo_s, mk_s, mv_s, *, tiles_per_batch):
    @pl.when(pl.program_id(0) == 0)
    def _():
        wq_s[...] = wq_ref[...].astype(BF16)
        wo_s[...] = wo_ref[...].astype(BF16)

    @pl.when(pl.program_id(0) % tiles_per_batch == 0)
    def _():
        for hh in range(X_HEADS):
            mk_s[hh] = mk_ref[:, hh, :].astype(BF16)
            mv_s[hh] = mv_ref[:, hh, :].astype(BF16)

    x = x_ref[...]
    h = _rms(x, g_ref[...]).astype(BF16)
    q = _dot(h, wq_s[...])
    heads = []
    for hh in range(X_HEADS):
        sl = slice(hh * X_HEAD_DIM, (hh + 1) * X_HEAD_DIM)
        s = lax.dot_general(q[:, sl].astype(BF16), mk_s[hh], NT_DIMS, preferred_element_type=F32)
        s = s * X_HEAD_DIM ** -0.5
        e = jnp.exp(s - jnp.max(s, axis=-1, keepdims=True))
        a = e / jnp.sum(e, axis=-1, keepdims=True)
        heads.append(_dot(a.astype(BF16), mv_s[hh]))
    o = jnp.concatenate(heads, axis=1).astype(BF16)
    o_ref[...] = x + _dot(o, wo_s[...])


def _xattn(x, g, w_q, w_o, mem_k, mem_v, layer, rows_per_batch, *, tm_cap=512):
    m, d = x.shape
    xw = w_q.shape[-1]
    n_mem = mem_k.shape[2]
    tm = _row_tile(rows_per_batch, tm_cap)
    tiles_per_batch = rows_per_batch // tm
    vmem = 4 * tm * d * 4 + 2 * 2 * d * xw * 4 + 2 * d * xw * 2 + 8 * n_mem * xw * 4 + 6 * tm * xw * 4
    mem_spec = pl.BlockSpec((None, None, n_mem, X_HEADS, X_HEAD_DIM),
                            lambda i: (layer, i // tiles_per_batch, 0, 0, 0))
    mem_scratch = pltpu.VMEM((X_HEADS, n_mem, X_HEAD_DIM), BF16)
    return pl.pallas_call(
        functools.partial(_xattn_body, tiles_per_batch=tiles_per_batch),
        grid=(m // tm,),
        in_specs=[
            pl.BlockSpec((tm, d), lambda i: (i, 0)),
            pl.BlockSpec((1, d), lambda i: (0, 0)),
            pl.BlockSpec((None, d, xw), lambda i: (layer, 0, 0)),
            pl.BlockSpec((None, xw, d), lambda i: (layer, 0, 0)),
            mem_spec,
            mem_spec,
        ],
        out_specs=pl.BlockSpec((tm, d), lambda i: (i, 0)),
        out_shape=jax.ShapeDtypeStruct((m, d), F32),
        scratch_shapes=[pltpu.VMEM((d, xw), BF16), pltpu.VMEM((xw, d), BF16), mem_scratch, mem_scratch],
        compiler_params=_params(vmem, 1),
        name="xattn",
    )(x, g.reshape(1, d), w_q, w_o, mem_k, mem_v)


def _memkv_body(m_ref, g_ref, wk_ref, wv_ref, k_ref, v_ref):
    mn = _rms(m_ref[...], g_ref[...]).astype(BF16)
    k = _dot(mn, wk_ref[...].astype(BF16))
    v = _dot(mn, wv_ref[...].astype(BF16))
    for hh in range(X_HEADS):
        sl = slice(hh * X_HEAD_DIM, (hh + 1) * X_HEAD_DIM)
        k_ref[:, hh, :] = k[:, sl]
        v_ref[:, hh, :] = v[:, sl]


def _memkv(mem, norm_mem, w_xk, w_xv):
    depth, d, xw = w_xk.shape
    m = mem.shape[0]
    out = jax.ShapeDtypeStruct((depth, m, X_HEADS, X_HEAD_DIM), F32)
    vmem = 2 * m * d * 4 + m * d * 2 + 4 * d * xw * 4 + 2 * d * xw * 2 + 6 * m * xw * 4
    return pl.pallas_call(
        _memkv_body,
        grid=(depth,),
        in_specs=[
            pl.BlockSpec((m, d), lambda l: (0, 0)),
            pl.BlockSpec((None, 1, d), lambda l: (l, 0, 0)),
            pl.BlockSpec((None, d, xw), lambda l: (l, 0, 0)),
            pl.BlockSpec((None, d, xw), lambda l: (l, 0, 0)),
        ],
        out_specs=[pl.BlockSpec((None, m, X_HEADS, X_HEAD_DIM), lambda l: (l, 0, 0, 0))] * 2,
        out_shape=[out, out],
        compiler_params=_params(vmem, 1),
        name="memkv",
    )(mem, norm_mem.reshape(depth, 1, d), w_xk, w_xv)


def _rmsnorm_body(x_ref, g_ref, o_ref):
    o_ref[...] = _rms(x_ref[...], g_ref[...])


def _rmsnorm(x, g, *, tm_cap=512):
    m, d = x.shape
    tm = _row_tile(m, tm_cap)
    return pl.pallas_call(
        _rmsnorm_body,
        grid=(m // tm,),
        in_specs=[pl.BlockSpec((tm, d), lambda i: (i, 0)), pl.BlockSpec((1, d), lambda i: (0, 0))],
        out_specs=pl.BlockSpec((tm, d), lambda i: (i, 0)),
        out_shape=jax.ShapeDtypeStruct((m, d), F32),
        compiler_params=_params(6 * tm * d * 4, 1),
        name="rmsnorm",
    )(x, g.reshape(1, d))


def _suffix_matrix(n):
    r = lax.broadcasted_iota(jnp.int32, (n, n), 0)
    c = lax.broadcasted_iota(jnp.int32, (n, n), 1)
    return jnp.where(r >= c, 1.0, 0.0).astype(BF16)


def _sb_suffix(z, mask, u_mat):
    nz = -z
    lk = jnp.minimum(nz, 0.0) - jnp.log2(1.0 + jnp.exp2(jnp.minimum(z, nz)))
    if mask is not None:
        lk = jnp.where(mask, lk, 0.0)
    hi, lo = _split_hi_lo(lk)
    return _dot(hi, u_mat) + _dot(lo, u_mat)


def _sb_weights(z, incl, mask, acc):
    w = jnp.exp2(z + incl + acc)
    return w if mask is None else jnp.where(mask, w, 0.0)


def _sb_tile(z, mask, acc, u_mat):
    incl = _sb_suffix(z, mask, u_mat)
    return _sb_weights(z, incl, mask, acc), acc + incl[:, 0:1]


def _sbp_body(bias_ref, q_ref, k_ref, v_ref, o_ref, *, tq, tk, scale):
    h = pl.program_id(1)
    i = pl.program_id(2)
    bias = bias_ref[h] * LOG2E
    q = (q_ref[...] * (scale * LOG2E)).astype(BF16)
    u_mat = _suffix_matrix(tk)
    n_diag = tq // tk
    nk = (i + 1) * n_diag

    def logits(q_rows, start):
        kb = k_ref[pl.ds(start, tk), :].astype(BF16)
        return lax.dot_general(q_rows, kb, NT_DIMS, preferred_element_type=F32) + bias

    def sweep(tiles, acc, out):
        incls = [_sb_suffix(z, mask, u_mat) for z, mask, _ in tiles]
        for (z, mask, start), incl in zip(tiles, incls):
            w = _sb_weights(z, incl, mask, acc).astype(BF16)
            acc = acc + incl[:, 0:1]
            out = out + _dot(w, v_ref[pl.ds(start, tk), :].astype(BF16))
        return acc, out

    def diagonal_group():
        base = i * tq
        tri = lax.broadcasted_iota(jnp.int32, (tk, tk), 1) < lax.broadcasted_iota(jnp.int32, (tk, tk), 0)
        starts = [pl.multiple_of(base + c * tk, tk) for c in range(n_diag)]
        zs = {(s, c): logits(q[s * tk:(s + 1) * tk], starts[c]) for s in range(n_diag) for c in range(s + 1)}
        accs, outs = [], []
        for s in range(n_diag):
            tiles = [(zs[s, c], tri if c == s else None, starts[c]) for c in range(s, -1, -1)]
            acc, out = sweep(tiles, jnp.zeros((tk, 1), F32), jnp.zeros((tk, dh), F32))
            accs.append(acc)
            outs.append(out)
        return jnp.concatenate(accs, axis=0), jnp.concatenate(outs, axis=0)

    def unmasked_group(m, carry):
        starts = [pl.multiple_of((nk - 1 - (n_diag * m + r)) * tk, tk) for r in range(n_diag)]
        return sweep([(logits(q, s), None, s) for s in starts], *carry)

    dh = q_ref.shape[1]
    _, out = lax.fori_loop(1, i + 1, unmasked_group, diagonal_group())
    o_ref[...] = out.astype(o_ref.dtype)


def _sb_attend_prompt(qkv, bias, nb, t, *, tq=512):
    m, d3 = qkv.shape
    d = d3 // 3
    dh = d // SB_HEADS
    tq = min(tq, t)
    tk = min(SB_KEY_TILE, tq)
    nq = t // tq
    vmem = 4 * t * dh * 4 + 4 * tq * dh * 4 + 16 * tq * tk * 4
    return pl.pallas_call(
        functools.partial(_sbp_body, tq=tq, tk=tk, scale=dh ** -0.5),
        grid=(nb, SB_HEADS, nq),
        in_specs=[
            pl.BlockSpec(memory_space=pltpu.SMEM),
            pl.BlockSpec((tq, dh), lambda b, h, i: (b * nq + i, h)),
            pl.BlockSpec((t, dh), lambda b, h, i: (b, SB_HEADS + h)),
            pl.BlockSpec((t, dh), lambda b, h, i: (b, 2 * SB_HEADS + h)),
        ],
        out_specs=pl.BlockSpec((tq, dh), lambda b, h, i: (b * nq + i, h)),
        out_shape=jax.ShapeDtypeStruct((m, d), _mxu_feed_dtype(tq)),
        compiler_params=_params(vmem, 3),
        name="sb_prompt",
    )(bias, qkv, qkv, qkv)


def _sbs_body(pt_ref, bias_ref, qkv_ref, *refs, t, dh, page, npp, scale):
    del pt_ref
    k_refs, v_refs = refs[:npp], refs[npp:2 * npp]
    o_ref, out_s, acc_s = refs[2 * npp:]
    s = pl.program_id(1)
    d = SB_HEADS * dh
    rows = SB_HEADS * t
    u_mat = _suffix_matrix(page)

    def attend(pages, mask, acc, outs):
        qs = [(qkv_ref[:, h * dh:(h + 1) * dh] * (scale * LOG2E)).astype(BF16) for h in range(SB_HEADS)]
        zs = [jnp.concatenate([lax.dot_general(qs[h], get_k(h), NT_DIMS, preferred_element_type=F32)
                               + bias_ref[h] * LOG2E for h in range(SB_HEADS)], axis=0)
              for get_k, _ in pages]
        incls = [_sb_suffix(z, mask, u_mat) for z in zs]
        ws = []
        for z, incl in zip(zs, incls):
            ws.append(_sb_weights(z, incl, mask, acc))
            acc = acc + incl[:, 0:1]
        for w, (_, get_v) in zip(ws, pages):
            outs = [outs[h] + _dot(w[h * t:(h + 1) * t, :].astype(BF16), get_v(h)) for h in range(SB_HEADS)]
        return acc, outs

    def store(acc, outs):
        acc_s[...] = acc
        for h in range(SB_HEADS):
            out_s[h * t:(h + 1) * t, :] = outs[h]

    @pl.when(s == 0)
    def _():
        pad = jnp.zeros((page - t, dh), F32)
        key = lax.broadcasted_iota(jnp.int32, (rows, page), 1)
        qry = lax.broadcasted_iota(jnp.int32, (rows, page), 0) % t
        new_tokens = (
            lambda h: jnp.concatenate([qkv_ref[:, d + h * dh:d + (h + 1) * dh], pad], axis=0).astype(BF16),
            lambda h: jnp.concatenate([qkv_ref[:, 2 * d + h * dh:2 * d + (h + 1) * dh], pad], axis=0).astype(BF16))
        acc, outs = attend([new_tokens], key < qry, jnp.zeros((rows, 1), F32),
                           [jnp.zeros((t, dh), F32)] * SB_HEADS)
        store(acc, outs)

    def cached_page(j):
        return (lambda h: k_refs[j][pl.ds(h, page, stride=SB_HEADS), :].astype(BF16),
                lambda h: v_refs[j][pl.ds(h, page, stride=SB_HEADS), :].astype(BF16))

    acc, outs = attend([cached_page(j) for j in range(npp)], None, acc_s[...],
                       [out_s[h * t:(h + 1) * t, :] for h in range(SB_HEADS)])
    store(acc, outs)

    @pl.when(s == pl.num_programs(1) - 1)
    def _():
        for h in range(SB_HEADS):
            o_ref[:, h * dh:(h + 1) * dh] = outs[h]


def _sb_attend_sample(qkv, bias, cache_k, cache_v, page_table, nb, t, *, npp=8):
    m, d3 = qkv.shape
    d = d3 // 3
    dh = d // SB_HEADS
    n_phys, page = cache_k.shape[0], cache_k.shape[1]
    n_pages = page_table.shape[1]
    while n_pages % npp:
        npp //= 2
    ck = cache_k.reshape(n_phys, page * SB_HEADS, dh)
    cv = cache_v.reshape(n_phys, page * SB_HEADS, dh)

    def cache_spec(j):
        return pl.BlockSpec((None, page * SB_HEADS, dh),
                            lambda b, s, pt: (pt[b, n_pages - 1 - (s * npp + j)], 0, 0))

    vmem = 4 * npp * page * d * 4 + 2 * t * d3 * 4 + 24 * npp * SB_HEADS * t * page * 4
    return pl.pallas_call(
        functools.partial(_sbs_body, t=t, dh=dh, page=page, npp=npp, scale=dh ** -0.5),
        grid_spec=pltpu.PrefetchScalarGridSpec(
            num_scalar_prefetch=1,
            grid=(nb, n_pages // npp),
            in_specs=[pl.BlockSpec(memory_space=pltpu.SMEM), pl.BlockSpec((t, d3), lambda b, s, pt: (b, 0))]
            + [cache_spec(j) for j in range(npp)] * 2,
            out_specs=pl.BlockSpec((t, d), lambda b, s, pt: (b, 0)),
            scratch_shapes=[pltpu.VMEM((SB_HEADS * t, dh), F32), pltpu.VMEM((SB_HEADS * t, 1), F32)],
        ),
        out_shape=jax.ShapeDtypeStruct((m, d), F32),
        compiler_params=_params(vmem, 2),
        name="sb_sample",
    )(page_table, bias, qkv, *([ck] * npp), *([cv] * npp))


def _cmul(ar, ai, br, bi):
    return ar * br - ai * bi, ar * bi + ai * br


def _gelu_tanh(x):
    return 0.5 * x * (1.0 + jnp.tanh(math.sqrt(2.0 / math.pi) * (x + 0.044715 * (x * x * x))))


def _s5_body(h_ref, ar_ref, ai_ref, br_ref, bi_ref, cr_ref, ci_ref, d_ref, s0r_ref, s0i_ref,
             y_ref, sr_ref, si_ref, xr_s, xi_s, pr_s, pi_s, cr_s, ci_s, *, nbk):
    t = pl.program_id(2)
    rows8, n_state = pr_s.shape
    row = lax.broadcasted_iota(jnp.int32, (rows8, n_state), 0)

    def last_step_to_all_rows(x):
        x = jnp.where(row >= rows8 - nbk, x, 0.0)
        have = nbk
        while have < rows8:
            x = x + pltpu.roll(x, rows8 - have, 0)
            have *= 2
        return x

    @pl.when(t == 0)
    def _():
        ar, ai = ar_ref[...], ai_ref[...]
        pr, pi = ar, ai
        for r in range(rows8):
            if r and r % nbk == 0:
                pr, pi = _cmul(pr, pi, ar, ai)
            pr_s[r:r + 1, :] = pr
            pi_s[r:r + 1, :] = pi
        cr_s[...] = jnp.zeros_like(cr_s)
        ci_s[...] = jnp.zeros_like(ci_s)
        cr_s[rows8 - nbk:, :] = s0r_ref[...]
        ci_s[rows8 - nbk:, :] = s0i_ref[...]
        cr_s[...] = last_step_to_all_rows(cr_s[...])
        ci_s[...] = last_step_to_all_rows(ci_s[...])

    tt = h_ref.shape[1]
    n_slab = xr_s.shape[0]
    lanes = [slice(s * V7X_LANES, (s + 1) * V7X_LANES) for s in range(n_slab)]

    def batch_rows(b):
        return pl.ds(b, tt, stride=nbk) if nbk > 1 else pl.ds(0, tt)

    b_r, b_i = br_ref[...].astype(BF16), bi_ref[...].astype(BF16)
    for b in range(nbk):
        u = h_ref[b].astype(BF16)
        bu_r, bu_i = _dot(u, b_r), _dot(u, b_i)
        for s in range(n_slab):
            xr_s[s, batch_rows(b), :] = bu_r[:, lanes[s]]
            xi_s[s, batch_rows(b), :] = bu_i[:, lanes[s]]
    pw_r, pw_i = pr_s[...], pi_s[...]

    def tile(blk, carry):
        c_r, c_i = carry
        rows = pl.ds(pl.multiple_of(blk * rows8, rows8), rows8)
        x_r = jnp.concatenate([xr_s[s, rows, :] for s in range(n_slab)], axis=1)
        x_i = jnp.concatenate([xi_s[s, rows, :] for s in range(n_slab)], axis=1)
        dist = 1
        while dist * nbk < rows8:
            lo = (dist - 1) * nbk
            a_r, a_i = pw_r[lo:lo + 1, :], pw_i[lo:lo + 1, :]
            keep = row >= dist * nbk
            sh_r = jnp.where(keep, pltpu.roll(x_r, dist * nbk, 0), 0.0)
            sh_i = jnp.where(keep, pltpu.roll(x_i, dist * nbk, 0), 0.0)
            d_r, d_i = _cmul(a_r, a_i, sh_r, sh_i)
            x_r, x_i = x_r + d_r, x_i + d_i
            dist *= 2
        d_r, d_i = _cmul(pw_r, pw_i, c_r, c_i)
        x_r, x_i = x_r + d_r, x_i + d_i
        for s in range(n_slab):
            xr_s[s, rows, :] = x_r[:, lanes[s]]
            xi_s[s, rows, :] = x_i[:, lanes[s]]
        return last_step_to_all_rows(x_r), last_step_to_all_rows(x_i)

    c_r, c_i = lax.fori_loop(0, xr_s.shape[1] // rows8, tile, (cr_s[...], ci_s[...]))
    cr_s[...] = c_r
    ci_s[...] = c_i

    c_re, c_im = cr_ref[...].astype(BF16), ci_ref[...].astype(BF16)
    for b in range(nbk):
        st_r = jnp.concatenate([xr_s[s, batch_rows(b), :] for s in range(n_slab)], axis=1).astype(BF16)
        st_i = jnp.concatenate([xi_s[s, batch_rows(b), :] for s in range(n_slab)], axis=1).astype(BF16)
        y = _dot(st_r, c_re) - _dot(st_i, c_im)
        y_ref[b] = _gelu_tanh(y + d_ref[...] * h_ref[b]).astype(y_ref.dtype)

    @pl.when(t == pl.num_programs(2) - 1)
    def _():
        sr_ref[...] = c_r[rows8 - nbk:, :]
        si_ref[...] = c_i[rows8 - nbk:, :]


def _s5_discretize(a_re, a_im, log_dt, b_re, b_im, c_re, c_im):
    g, p = a_re.shape
    gb = S5_CH_BLOCK // S5_GROUP_CH
    nblk = g // gb
    dt = jnp.exp(log_dt)[:, None]
    mag = jnp.exp(a_re * dt)
    abar_r, abar_i = mag * jnp.cos(a_im * dt), mag * jnp.sin(a_im * dt)
    xr, xi = abar_r - 1.0, abar_i
    den = a_re * a_re + a_im * a_im
    coef_r = (xr * a_re + xi * a_im) / den
    coef_i = (xi * a_re - xr * a_im) / den
    bbar_r = coef_r[..., None] * b_re - coef_i[..., None] * b_im
    bbar_i = coef_r[..., None] * b_im + coef_i[..., None] * b_re
    eye = jnp.eye(gb, dtype=F32)

    def b_big(bb):
        blk = bb.reshape(nblk, gb, p, S5_GROUP_CH).transpose(0, 1, 3, 2)
        return blk[:, :, :, None, :] * eye[None, :, None, :, None]

    def c_big(cc):
        blk = cc.reshape(nblk, gb, S5_GROUP_CH, p).transpose(0, 1, 3, 2)
        return blk[:, :, :, None, :] * eye[None, :, None, :, None]

    n_state = gb * p
    return (abar_r.reshape(1, g * p), abar_i.reshape(1, g * p),
            b_big(bbar_r).reshape(nblk, S5_CH_BLOCK, n_state), b_big(bbar_i).reshape(nblk, S5_CH_BLOCK, n_state),
            c_big(c_re).reshape(nblk, n_state, S5_CH_BLOCK), c_big(c_im).reshape(nblk, n_state, S5_CH_BLOCK))


def _s5_core(h, disc, d_skip, s0r, s0i, nb, t, *, tt_cap=256):
    m, d = h.shape
    nbk = math.gcd(nb, V7X_SUBLANES)
    abar_r, abar_i, bbr, bbi, ccr, cci = disc
    nblk, cb, n_state = bbr.shape
    tt = _row_tile(t, tt_cap)
    nt = t // tt
    ng = nb // nbk
    state = jax.ShapeDtypeStruct((ng, nbk, nblk * n_state), F32)
    lane_spec = pl.BlockSpec((1, n_state), lambda b, c, i: (0, c))
    state_spec = pl.BlockSpec((None, nbk, n_state), lambda b, c, i: (b, 0, c))
    b_spec = pl.BlockSpec((None, cb, n_state), lambda b, c, i: (c, 0, 0))
    c_spec = pl.BlockSpec((None, n_state, cb), lambda b, c, i: (c, 0, 0))
    row_spec = pl.BlockSpec((nbk, tt, cb), lambda b, c, i: (b, i, c))
    vmem = 4 * tt * nbk * cb * 4 + 16 * cb * n_state * 4 + 8 * tt * nbk * n_state * 4
    lane_tile = pltpu.VMEM((V7X_SUBLANES, n_state), F32)
    slabs = pltpu.VMEM((n_state // V7X_LANES, tt * nbk, V7X_LANES), F32)
    y, sr, si = pl.pallas_call(
        functools.partial(_s5_body, nbk=nbk),
        grid=(ng, nblk, nt),
        in_specs=[row_spec, lane_spec, lane_spec, b_spec, b_spec, c_spec, c_spec,
                  pl.BlockSpec((1, cb), lambda b, c, i: (0, c)), state_spec, state_spec],
        out_specs=[row_spec, state_spec, state_spec],
        out_shape=[jax.ShapeDtypeStruct((nb, t, d), _mxu_feed_dtype(tt)), state, state],
        scratch_shapes=[slabs, slabs, lane_tile, lane_tile, lane_tile, lane_tile],
        compiler_params=_params(vmem, 3),
        name="s5",
    )(h.reshape(nb, t, d), abar_r, abar_i, bbr, bbi, ccr, cci, d_skip.reshape(1, d),
      s0r.reshape(ng, nbk, -1), s0i.reshape(ng, nbk, -1))
    return y.reshape(m, d), sr.reshape(nb, -1), si.reshape(nb, -1)


def _conv_body(b_ref, c_ref, v_ref, w_ref, p_ref, y_ref, s_ref):
    z = c_ref[...] * v_ref[...]
    t = z.shape[0]
    row = lax.broadcasted_iota(jnp.int32, z.shape, 0)
    p0, p1 = p_ref[0:1, :], p_ref[1:2, :]
    z1 = jnp.where(row >= 1, pltpu.roll(z, 1, 0), p1)
    z2 = jnp.where(row >= 2, pltpu.roll(z, 2, 0), jnp.where(row == 0, p0, p1))
    y = w_ref[0:1, :] * z2 + w_ref[1:2, :] * z1 + w_ref[2:3, :] * z
    y_ref[...] = (b_ref[...] * y).astype(y_ref.dtype)
    s_ref[...] = z[t - (CONV_W - 1):, :]


def _conv_core(bcv, w_conv, prev, nb, t, *, tc=512):
    m, d3 = bcv.shape
    d = d3 // 3
    nc = d // tc
    vmem = 12 * t * tc * 4
    return pl.pallas_call(
        _conv_body,
        grid=(nb, nc),
        in_specs=[
            pl.BlockSpec((t, tc), lambda b, j: (b, j)),
            pl.BlockSpec((t, tc), lambda b, j: (b, nc + j)),
            pl.BlockSpec((t, tc), lambda b, j: (b, 2 * nc + j)),
            pl.BlockSpec((CONV_W, tc), lambda b, j: (0, j)),
            pl.BlockSpec((None, CONV_W - 1, tc), lambda b, j: (b, 0, j)),
        ],
        out_specs=[pl.BlockSpec((t, tc), lambda b, j: (b, j)),
                   pl.BlockSpec((None, CONV_W - 1, tc), lambda b, j: (b, 0, j))],
        out_shape=[jax.ShapeDtypeStruct((m, d), _mxu_feed_dtype(t)), jax.ShapeDtypeStruct((nb, CONV_W - 1, d), F32)],
        compiler_params=_params(vmem, 2),
        name="conv",
    )(bcv, bcv, bcv, w_conv, prev)


def _gla_gate_body(x_ref, g_ref, w1_ref, w2_ref, b_ref, o_ref):
    h = _rms(x_ref[...], g_ref[...]).astype(BF16)
    low = _dot(h, w1_ref[...].astype(BF16))
    y = _dot(low.astype(BF16), w2_ref[...].astype(BF16)) + b_ref[...]
    o_ref[...] = -_softplus(-y) / GLA_TAU


def _gla_gate(x, g, w1, w2, b, *, tm_cap=512):
    m, d = x.shape
    r = w1.shape[1]
    gk = w2.shape[1]
    rp = V7X_LANES
    w1p = jnp.pad(w1, ((0, 0), (0, rp - r)))
    w2p = jnp.pad(w2, ((0, rp - r), (0, 0)))
    tm = _row_tile(m, tm_cap)
    vmem = 3 * tm * d * 4 + 4 * d * rp * 4 + 4 * rp * gk * 4 + 6 * tm * gk * 4
    return pl.pallas_call(
        _gla_gate_body,
        grid=(m // tm,),
        in_specs=[
            pl.BlockSpec((tm, d), lambda i: (i, 0)),
            pl.BlockSpec((1, d), lambda i: (0, 0)),
            pl.BlockSpec((d, rp), lambda i: (0, 0)),
            pl.BlockSpec((rp, gk), lambda i: (0, 0)),
            pl.BlockSpec((1, gk), lambda i: (0, 0)),
        ],
        out_specs=pl.BlockSpec((tm, gk), lambda i: (i, 0)),
        out_shape=jax.ShapeDtypeStruct((m, gk), F32),
        compiler_params=_params(vmem, 1),
        name="gla_gate",
    )(x, g.reshape(1, d), w1p, w2p, b.reshape(1, gk))


def _cumsum_rows(x):
    n = x.shape[0]
    row = lax.broadcasted_iota(jnp.int32, x.shape, 0)
    dist = 1
    while dist < n:
        x = x + jnp.where(row >= dist, pltpu.roll(x, dist, 0), 0.0)
        dist *= 2
    return x


def _gla_chunk(q, k, v, glog, st):
    c, dk = q.shape
    b = _cumsum_rows(glog)
    o = lax.dot_general((q * jnp.exp(b)).astype(BF16), st.astype(BF16), NT_DIMS, preferred_element_type=F32)
    rowc = lax.broadcasted_iota(jnp.int32, (c, 1), 0)
    lane = lax.broadcasted_iota(jnp.int32, (1, c), 1)
    rsub = lax.broadcasted_iota(jnp.int32, (GLA_SUB, 1), 0)
    att_rows = []
    for blk in range(c // GLA_SUB):
        lo = blk * GLA_SUB
        b_i = b[lo:lo + GLA_SUB]
        q_i = q[lo:lo + GLA_SUB]
        if blk > 0:
            b_0 = b[lo - 1:lo]
            q_off = (q_i * jnp.exp(b_i - b_0)).astype(BF16)
            k_off = (k * jnp.exp(jnp.where(rowc < lo, b_0 - b, NEG_INF))).astype(BF16)
            att = lax.dot_general(q_off, k_off, NT_DIMS, preferred_element_type=F32)
        else:
            att = jnp.zeros((GLA_SUB, c), F32)
        for jj in range(GLA_SUB):
            j = lo + jj
            dec = jnp.exp(jnp.where(rsub >= jj, b_i - b[j:j + 1], NEG_INF))
            col = jnp.sum(q_i * k[j:j + 1] * dec, axis=1, keepdims=True)
            att = att + jnp.where(lane == j, col, 0.0)
        att_rows.append(att)
    att = att_rows[0] if len(att_rows) == 1 else jnp.concatenate(att_rows, axis=0)
    vb = v.astype(BF16)
    o = o + _dot(att.astype(BF16), vb)
    b_last = b[c - 1:c]
    k_dec = (k * jnp.exp(b_last - b)).astype(BF16)
    st = st * jnp.exp(b_last) + lax.dot_general(vb, k_dec, TN_DIMS, preferred_element_type=F32)
    return o, st


def _gla_body(*refs, chunk, has_prev, scale):
    refs = list(refs)
    q_ref, k_ref, v_ref, r_ref, gl_ref, ng_ref = refs[:6]
    s0_ref = refs[6] if has_prev else None
    o_ref, s_ref, st_s = refs[-3:]
    t = pl.program_id(1)
    n_heads, dv, dk = st_s.shape

    @pl.when(t == 0)
    def _():
        for h in range(n_heads):
            st_s[h] = s0_ref[h].T if has_prev else jnp.zeros((dv, dk), F32)

    def step(ci, carry):
        rows = pl.ds(pl.multiple_of(ci * chunk, chunk), chunk)
        for h in range(n_heads):
            ck, cv = slice(h * dk, (h + 1) * dk), slice(h * dv, (h + 1) * dv)
            o, st = _gla_chunk(q_ref[rows, ck] * scale, k_ref[rows, ck], v_ref[rows, cv], gl_ref[rows, ck], st_s[h])
            st_s[h] = st
            o = o * lax.rsqrt(jnp.mean(o * o, axis=-1, keepdims=True) + EPS) * ng_ref[...]
            o_ref[rows, cv] = (o * jax.nn.silu(r_ref[rows, cv])).astype(o_ref.dtype)
        return carry

    lax.fori_loop(0, q_ref.shape[0] // chunk, step, 0)

    @pl.when(t == pl.num_programs(1) - 1)
    def _():
        for h in range(n_heads):
            s_ref[h] = st_s[h].T


def _gla_core(qkvr, glog, norm_g, prev, nb, t, *, tt_cap=256):
    m = qkvr.shape[0]
    gk = glog.shape[1]
    dk = gk // GLA_HEADS
    d = (qkvr.shape[1] - 2 * gk) // 2
    dv = d // GLA_HEADS
    chunk = GLA_CHUNK if t % GLA_CHUNK == 0 else t
    tt = _row_tile(t, tt_cap)
    nt = t // tt
    in_specs = [
        pl.BlockSpec((tt, gk), lambda b, i: (b * nt + i, 0)),
        pl.BlockSpec((tt, gk), lambda b, i: (b * nt + i, 1)),
        pl.BlockSpec((tt, d), lambda b, i: (b * nt + i, 2 * gk // d)),
        pl.BlockSpec((tt, d), lambda b, i: (b * nt + i, 2 * gk // d + 1)),
        pl.BlockSpec((tt, gk), lambda b, i: (b * nt + i, 0)),
        pl.BlockSpec((1, dv), lambda b, i: (0, 0)),
    ]
    args = [qkvr, qkvr, qkvr, qkvr, glog, norm_g.reshape(1, dv)]
    state_spec = pl.BlockSpec((None, GLA_HEADS, dk, dv), lambda b, i: (b, 0, 0, 0))
    if prev is not None:
        in_specs.append(state_spec)
        args.append(prev)
    vmem = 4 * tt * (3 * gk + 3 * d) * 4 + 7 * GLA_HEADS * dk * dv * 4 + 64 * GLA_HEADS * chunk * dv * 4
    return pl.pallas_call(
        functools.partial(_gla_body, chunk=chunk, has_prev=prev is not None, scale=dk ** -0.5),
        grid=(nb, nt),
        in_specs=in_specs,
        out_specs=[pl.BlockSpec((tt, d), lambda b, i: (b * nt + i, 0)), state_spec],
        out_shape=[jax.ShapeDtypeStruct((m, d), _mxu_feed_dtype(chunk)),
                   jax.ShapeDtypeStruct((nb, GLA_HEADS, dk, dv), F32)],
        scratch_shapes=[pltpu.VMEM((GLA_HEADS, dv, dk), F32)],
        compiler_params=_params(vmem, 2),
        name="gla",
    )(*args)


def _trunk(xp, xs, prompt_dims, sample_dims, mem_p, mem_s, p, s5_disc, sample):
    (nbp, tp), (nbs, ts) = prompt_dims, sample_dims
    d = xp.shape[1]
    depth = p["norm_mix"].shape[0]
    new_p, new_s = {}, {}
    for i in range(depth):
        g_mix = p["norm_mix"][i]
        kind = i % 4
        if kind == 0:
            qkv_p, qkv_s = _mm(xp, p["w_sb_qkv"], g=g_mix, small=xs)
            o_p = _sb_attend_prompt(qkv_p, p["sb_bias"], nbp, tp)
            o_s = _sb_attend_sample(qkv_s, p["sb_bias"], sample["cache_sb_k"], sample["cache_sb_v"],
                                    sample["page_table"], nbs, ts)
            xp, xs = _mm(o_p, p["w_sb_out"], res=xp, small=o_s, small_res=xs)
            for new, qkv, nb, t in ((new_p, qkv_p, nbp, tp), (new_s, qkv_s, nbs, ts)):
                shp = (nb, t, SB_HEADS, d // SB_HEADS)
                new["sb"] = (qkv[:, d:2 * d].reshape(shp), qkv[:, 2 * d:].reshape(shp))
        elif kind == 1:
            n_state = d // S5_GROUP_CH * S5_STATE
            zeros = jnp.zeros((nbp, n_state), F32)
            y_p, sr_p, si_p = _s5_core(_rmsnorm(xp, g_mix), s5_disc, p["s5_d"], zeros, zeros, nbp, tp)
            y_s, sr_s, si_s = _s5_core(_rmsnorm(xs, g_mix), s5_disc, p["s5_d"], sample["state_s5_re"],
                                       sample["state_s5_im"], nbs, ts)
            xp, xs = _mm(y_p, p["w_s5_glu"], res=xp, glu=True, small=y_s, small_res=xs)
            for new, sr, si, nb in ((new_p, sr_p, si_p, nbp), (new_s, sr_s, si_s, nbs)):
                sshape = (nb, d // S5_GROUP_CH, S5_STATE)
                new["s5"] = (sr.reshape(sshape), si.reshape(sshape))
        elif kind == 2:
            bcv_p, bcv_s = _mm(xp, p["w_conv_in"], g=g_mix, small=xs)
            yg_p, new_p["conv"] = _conv_core(bcv_p, p["w_conv"], jnp.zeros((nbp, CONV_W - 1, d), F32), nbp, tp)
            yg_s, new_s["conv"] = _conv_core(bcv_s, p["w_conv"], sample["state_conv"], nbs, ts)
            xp, xs = _mm(yg_p, p["w_conv_out"], res=xp, small=yg_s, small_res=xs)
        else:
            qkvr_p, qkvr_s = _mm(xp, p["w_gla_in"], g=g_mix, small=xs)
            gate = (g_mix, p["w_gla_g1"], p["w_gla_g2"], p["b_gla_g"])
            o_p, new_p["gla"] = _gla_core(qkvr_p, _gla_gate(xp, *gate), p["gla_norm"], None, nbp, tp)
            o_s, new_s["gla"] = _gla_core(qkvr_s, _gla_gate(xs, *gate), p["gla_norm"], sample["state_gla"], nbs, ts)
            xp, xs = _mm(o_p, p["w_gla_out"], res=xp, small=o_s, small_res=xs)
        xp = _xattn(xp, p["norm_xattn"][i], p["w_xq"], p["w_xo"], mem_p[0], mem_p[1], i, tp)
        xs = _xattn(xs, p["norm_xattn"][i], p["w_xq"], p["w_xo"], mem_s[0], mem_s[1], i, ts)
        xp, xs = _ffn(xp, p["norm_ffn"][i], p["w_ffn_in"], p["w_ffn_out"], i, small=xs,
                      final_g=p["norm_final"] if i == depth - 1 else None)
    return xp.reshape(nbp, tp, d), xs.reshape(nbs, ts, d), new_p, new_s


def kernel(x_prompt, x_sample, mem_prompt, cache_sb_k, cache_sb_v, page_table, state_s5_re, state_s5_im, state_conv, state_gla, cache_mem_k, cache_mem_v, norm_mix, norm_xattn, norm_mem, norm_ffn, norm_final, w_sb_qkv, w_sb_out, sb_bias, s5_a_re, s5_a_im, s5_log_dt, s5_b_re, s5_b_im, s5_c_re, s5_c_im, s5_d, w_s5_glu, w_conv_in, w_conv, w_conv_out, w_gla_in, w_gla_g1, w_gla_g2, b_gla_g, gla_norm, w_gla_out, w_xq, w_xk, w_xv, w_xo, w_ffn_in, w_ffn_out):
    p = dict(norm_mix=norm_mix, norm_xattn=norm_xattn, norm_ffn=norm_ffn, norm_final=norm_final,
             w_sb_qkv=w_sb_qkv, w_sb_out=w_sb_out, sb_bias=sb_bias, s5_d=s5_d, w_s5_glu=w_s5_glu,
             w_conv_in=w_conv_in, w_conv=w_conv, w_conv_out=w_conv_out,
             w_gla_in=w_gla_in, w_gla_g1=w_gla_g1, w_gla_g2=w_gla_g2, b_gla_g=b_gla_g,
             gla_norm=gla_norm, w_gla_out=w_gla_out, w_xq=w_xq, w_xo=w_xo,
             w_ffn_in=w_ffn_in, w_ffn_out=w_ffn_out)
    nbp, tp, d = x_prompt.shape
    nbs, ts, _ = x_sample.shape
    depth = norm_mix.shape[0]
    n_mem = mem_prompt.shape[1]
    mshape = (depth, nbp, n_mem, X_HEADS, X_HEAD_DIM)
    mem_k_p, mem_v_p = _memkv(mem_prompt.reshape(nbp * n_mem, d), norm_mem, w_xk, w_xv)
    mem_k_p, mem_v_p = mem_k_p.reshape(mshape), mem_v_p.reshape(mshape)
    s5_disc = _s5_discretize(s5_a_re, s5_a_im, s5_log_dt, s5_b_re, s5_b_im, s5_c_re, s5_c_im)

    sample = dict(cache_sb_k=cache_sb_k, cache_sb_v=cache_sb_v, page_table=page_table,
                  state_s5_re=state_s5_re.reshape(nbs, -1), state_s5_im=state_s5_im.reshape(nbs, -1),
                  state_conv=state_conv, state_gla=state_gla)
    y_p, y_s, new_p, new_s = _trunk(x_prompt.reshape(nbp * tp, d), x_sample.reshape(nbs * ts, d),
                                    (nbp, tp), (nbs, ts), (mem_k_p, mem_v_p), (cache_mem_k, cache_mem_v),
                                    p, s5_disc, sample)

    return (y_p, y_s, new_p["sb"][0], new_p["sb"][1], new_s["sb"][0], new_s["sb"][1],
            new_p["s5"][0], new_p["s5"][1], new_s["s5"][0], new_s["s5"][1],
            new_p["conv"], new_s["conv"], new_p["gla"], new_s["gla"],
            mem_k_p, mem_v_p)
```

```python
import functools
import math

import jax
import jax.numpy as jnp
from jax import lax
from jax.experimental import pallas as pl
from jax.experimental.pallas import tpu as pltpu

F32 = jnp.float32
BF16 = jnp.bfloat16
EPS = 1e-6
NEG_INF = float("-inf")
LOG2E = 1.4426950408889634

V7X_VMEM_BYTES = 64 * 1024 * 1024
V7X_SUBLANES = 8
V7X_LANES = 128

SB_HEADS = 16
SB_KEY_TILE = 256
S5_GROUP_CH = 16
S5_STATE = 64
S5_CH_BLOCK = 256
CONV_W = 3
GLA_HEADS = 4
GLA_TAU = 16.0
GLA_CHUNK = 32
GLA_SUB = 8
X_HEADS = 4
X_HEAD_DIM = 128

NT_DIMS = (((1,), (1,)), ((), ()))
TN_DIMS = (((0,), (0,)), ((), ()))


def _params(vmem_bytes, n_grid):
    limit = int(min(V7X_VMEM_BYTES - (6 << 20), max(vmem_bytes * 5 // 4 + (4 << 20), 16 << 20)))
    return pltpu.CompilerParams(dimension_semantics=("arbitrary",) * n_grid, vmem_limit_bytes=limit)


def _rms(x, g):
    return x * lax.rsqrt(jnp.mean(x * x, axis=-1, keepdims=True) + EPS) * g


def _softplus(z):
    return jnp.maximum(z, 0.0) + jnp.log1p(jnp.exp(-jnp.abs(z)))


def _dot(a, b):
    return jnp.dot(a, b, preferred_element_type=F32)


def _split_hi_lo(x):
    hi = x.astype(BF16)
    lo = (x - hi.astype(F32)).astype(BF16)
    return hi, lo


def _mxu_feed_dtype(rows_per_store):
    return BF16 if rows_per_store % (2 * V7X_SUBLANES) == 0 else F32


def _row_tile(m, cap):
    t = min(m, cap)
    while m % t:
        t //= 2
    return t


def _mm_body(*refs, norm, mode, dual):
    refs = list(refs)
    x_ref = refs.pop(0)
    g_ref = refs.pop(0) if norm else None
    w_ref = refs.pop(0)
    w2_ref = refs.pop(0) if mode == "glu" else None
    has_res = mode in ("res", "glu")
    res_ref = refs.pop(0) if has_res else None
    xs_ref = refs.pop(0) if dual else None
    ress_ref = refs.pop(0) if dual and has_res else None
    o_ref = refs.pop(0)
    os_ref = refs.pop(0) if dual else None
    lhs = refs.pop(0)
    tm = x_ref.shape[0]
    first_col = pl.program_id(1) == 0
    first_tile = pl.program_id(0) == 0

    def load_lhs(src, rows):
        x = src[...]
        if norm:
            x = _rms(x, g_ref[...])
        lhs[rows, :] = x.astype(BF16)

    def emit(n_rows):
        a = lhs[0:n_rows, :]
        y = _dot(a, w_ref[...].astype(BF16))
        if mode == "glu":
            y = y * jax.nn.sigmoid(_dot(a, w2_ref[...].astype(BF16)))
        o_ref[...] = res_ref[...] + y[0:tm] if has_res else y[0:tm]
        if n_rows > tm:
            os_ref[...] = ress_ref[...] + y[tm:n_rows] if has_res else y[tm:n_rows]

    pl.when(first_col)(lambda: load_lhs(x_ref, slice(0, tm)))
    if dual:
        n_all = lhs.shape[0]
        pl.when(jnp.logical_and(first_col, first_tile))(lambda: load_lhs(xs_ref, slice(tm, n_all)))
        pl.when(first_tile)(lambda: emit(n_all))
        pl.when(jnp.logical_not(first_tile))(lambda: emit(tm))
    else:
        emit(tm)


def _mm(x, w, *, g=None, res=None, glu=False, small=None, small_res=None, tm_cap=1024):
    m, k = x.shape
    n = w.shape[1] // (2 if glu else 1)
    mode = "glu" if glu else ("res" if res is not None else "plain")
    tn = 1024 if (mode == "plain" and n % 1024 == 0) else 512
    nj = n // tn
    dual = small is not None
    tm = _row_tile(m, tm_cap)
    in_specs = [pl.BlockSpec((tm, k), lambda i, j: (i, 0))]
    args = [x]
    if g is not None:
        in_specs.append(pl.BlockSpec((1, k), lambda i, j: (0, 0)))
        args.append(g.reshape(1, k))
    in_specs.append(pl.BlockSpec((k, tn), lambda i, j: (0, j)))
    args.append(w)
    if glu:
        in_specs.append(pl.BlockSpec((k, tn), lambda i, j: (0, j + nj)))
        args.append(w)
    if res is not None:
        in_specs.append(pl.BlockSpec((tm, tn), lambda i, j: (i, j)))
        args.append(res)
    out_specs = [pl.BlockSpec((tm, tn), lambda i, j: (i, j))]
    out_shape = [jax.ShapeDtypeStruct((m, n), F32)]
    ms = 0
    if dual:
        ms = small.shape[0]
        small_cols = pl.BlockSpec((ms, tn), lambda i, j: (0, jnp.where(i == 0, j, nj - 1)))
        in_specs.append(pl.BlockSpec((ms, k), lambda i, j: (0, 0)))
        args.append(small)
        if res is not None:
            in_specs.append(small_cols)
            args.append(small_res)
        out_specs.append(small_cols)
        out_shape.append(jax.ShapeDtypeStruct((ms, n), F32))
    vmem = (2 * (tm + ms) * k * 4 + (tm + ms) * k * 2 + (2 if glu else 1) * 2 * k * tn * 4
            + 4 * (tm + ms) * tn * 4 + 2 * k * tn * 4)
    outs = pl.pallas_call(
        functools.partial(_mm_body, norm=g is not None, mode=mode, dual=dual),
        grid=(m // tm, nj),
        in_specs=in_specs,
        out_specs=out_specs,
        out_shape=out_shape,
        scratch_shapes=[pltpu.VMEM((tm + ms, k), BF16)],
        compiler_params=_params(vmem, 2),
        name="mm_" + mode,
    )(*args)
    return tuple(outs) if dual else outs[0]


def _ffn_body(*refs, dual, final_norm):
    refs = list(refs)
    x_ref, g_ref, wg_ref, wu_ref, wo_ref = refs[:5]
    refs = refs[5:]
    gf_ref = refs.pop(0) if final_norm else None
    xs_ref = refs.pop(0) if dual else None
    o_ref = refs.pop(0)
    os_ref = refs.pop(0) if dual else None
    h_scr = refs.pop(0)
    tm = x_ref.shape[0]
    first_col = pl.program_id(1) == 0
    last_col = pl.program_id(1) == pl.num_programs(1) - 1
    first_tile = pl.program_id(0) == 0

    def start(src, rows, out):
        x = src[...]
        h_scr[rows, :] = _rms(x, g_ref[...]).astype(BF16)
        out[...] = x

    def accumulate(n_rows):
        h = h_scr[0:n_rows, :]
        gate = _dot(h, wg_ref[...].astype(BF16))
        up = _dot(h, wu_ref[...].astype(BF16))
        a = (jax.nn.silu(gate) * up).astype(BF16)
        y = _dot(a, wo_ref[...].astype(BF16))
        o_ref[...] += y[0:tm]
        if n_rows > tm:
            os_ref[...] += y[tm:n_rows]

    pl.when(first_col)(lambda: start(x_ref, slice(0, tm), o_ref))
    if dual:
        n_all = h_scr.shape[0]
        pl.when(jnp.logical_and(first_col, first_tile))(lambda: start(xs_ref, slice(tm, n_all), os_ref))
        pl.when(first_tile)(lambda: accumulate(n_all))
        pl.when(jnp.logical_not(first_tile))(lambda: accumulate(tm))
    else:
        accumulate(tm)
    if final_norm:
        @pl.when(last_col)
        def _():
            o_ref[...] = _rms(o_ref[...], gf_ref[...])

        if dual:
            @pl.when(jnp.logical_and(last_col, first_tile))
            def _():
                os_ref[...] = _rms(os_ref[...], gf_ref[...])


def _ffn(x, g, w_in, w_out, layer, *, small=None, final_g=None, tf=256, tm_cap=1024):
    m, d = x.shape
    f = w_out.shape[1]
    tm = _row_tile(m, tm_cap)
    nf = f // tf
    dual = small is not None
    ms = small.shape[0] if dual else 0
    in_specs = [
        pl.BlockSpec((tm, d), lambda i, j: (i, 0)),
        pl.BlockSpec((1, d), lambda i, j: (0, 0)),
        pl.BlockSpec((None, d, tf), lambda i, j: (layer, 0, j)),
        pl.BlockSpec((None, d, tf), lambda i, j: (layer, 0, j + nf)),
        pl.BlockSpec((None, tf, d), lambda i, j: (layer, j, 0)),
    ]
    args = [x, g.reshape(1, d), w_in, w_in, w_out]
    if final_g is not None:
        in_specs.append(pl.BlockSpec((1, d), lambda i, j: (0, 0)))
        args.append(final_g.reshape(1, d))
    out_specs = [pl.BlockSpec((tm, d), lambda i, j: (i, 0))]
    out_shape = [jax.ShapeDtypeStruct((m, d), F32)]
    if dual:
        in_specs.append(pl.BlockSpec((ms, d), lambda i, j: (0, 0)))
        args.append(small)
        out_specs.append(pl.BlockSpec((ms, d), lambda i, j: (0, 0)))
        out_shape.append(jax.ShapeDtypeStruct((ms, d), F32))
    vmem = (4 * (tm + ms) * d * 4 + (tm + ms) * d * 2 + 3 * 2 * d * tf * 4 + 3 * d * tf * 2
            + 3 * (tm + ms) * tf * 4)
    outs = pl.pallas_call(
        functools.partial(_ffn_body, dual=dual, final_norm=final_g is not None),
        grid=(m // tm, nf),
        in_specs=in_specs,
        out_specs=out_specs,
        out_shape=out_shape,
        scratch_shapes=[pltpu.VMEM((tm + ms, d), BF16)],
        compiler_params=_params(vmem, 2),
        name="ffn",
    )(*args)
    return tuple(outs) if dual else outs[0]


def _xattn_body(x_ref, g_ref, wq_ref, wo_ref, mk_ref, mv_ref, o_ref, wq_s, wo_s, mk_s, mv_s, *, tiles_per_batch):
    @pl.when(pl.program_id(0) == 0)
    def _():
        wq_s[...] = wq_ref[...].astype(BF16)
        wo_s[...] = wo_ref[...].astype(BF16)

    @pl.when(pl.program_id(0) % tiles_per_batch == 0)
    def _():
        for hh in range(X_HEADS):
            mk_s[hh] = mk_ref[:, hh, :].astype(BF16)
            mv_s[hh] = mv_ref[:, hh, :].astype(BF16)

    x = x_ref[...]
    h = _rms(x, g_ref[...]).astype(BF16)
    q = _dot(h, wq_s[...])
    heads = []
    for hh in range(X_HEADS):
        sl = slice(hh * X_HEAD_DIM, (hh + 1) * X_HEAD_DIM)
        s = lax.dot_general(q[:, sl].astype(BF16), mk_s[hh], NT_DIMS, preferred_element_type=F32)
        s = s * X_HEAD_DIM ** -0.5
        e = jnp.exp(s - jnp.max(s, axis=-1, keepdims=True))
        a = e / jnp.sum(e, axis=-1, keepdims=True)
        heads.append(_dot(a.astype(BF16), mv_s[hh]))
    o = jnp.concatenate(heads, axis=1).astype(BF16)
    o_ref[...] = x + _dot(o, wo_s[...])


def _xattn(x, g, w_q, w_o, mem_k, mem_v, layer, rows_per_batch, *, tm_cap=512):
    m, d = x.shape
    xw = w_q.shape[-1]
    n_mem = mem_k.shape[2]
    tm = _row_tile(rows_per_batch, tm_cap)
    tiles_per_batch = rows_per_batch // tm
    vmem = 4 * tm * d * 4 + 2 * 2 * d * xw * 4 + 2 * d * xw * 2 + 8 * n_mem * xw * 4 + 6 * tm * xw * 4
    mem_spec = pl.BlockSpec((None, None, n_mem, X_HEADS, X_HEAD_DIM),
                            lambda i: (layer, i // tiles_per_batch, 0, 0, 0))
    mem_scratch = pltpu.VMEM((X_HEADS, n_mem, X_HEAD_DIM), BF16)
    return pl.pallas_call(
        functools.partial(_xattn_body, tiles_per_batch=tiles_per_batch),
        grid=(m // tm,),
        in_specs=[
            pl.BlockSpec((tm, d), lambda i: (i, 0)),
            pl.BlockSpec((1, d), lambda i: (0, 0)),
            pl.BlockSpec((None, d, xw), lambda i: (layer, 0, 0)),
            pl.BlockSpec((None, xw, d), lambda i: (layer, 0, 0)),
            mem_spec,
            mem_spec,
        ],
        out_specs=pl.BlockSpec((tm, d), lambda i: (i, 0)),
        out_shape=jax.ShapeDtypeStruct((m, d), F32),
        scratch_shapes=[pltpu.VMEM((d, xw), BF16), pltpu.VMEM((xw, d), BF16), mem_scratch, mem_scratch],
        compiler_params=_params(vmem, 1),
        name="xattn",
    )(x, g.reshape(1, d), w_q, w_o, mem_k, mem_v)


def _memkv_body(m_ref, g_ref, wk_ref, wv_ref, k_ref, v_ref):
    mn = _rms(m_ref[...], g_ref[...]).astype(BF16)
    k = _dot(mn, wk_ref[...].astype(BF16))
    v = _dot(mn, wv_ref[...].astype(BF16))
    for hh in range(X_HEADS):
        sl = slice(hh * X_HEAD_DIM, (hh + 1) * X_HEAD_DIM)
        k_ref[:, hh, :] = k[:, sl]
        v_ref[:, hh, :] = v[:, sl]


def _memkv(mem, norm_mem, w_xk, w_xv):
    depth, d, xw = w_xk.shape
    m = mem.shape[0]
    out = jax.ShapeDtypeStruct((depth, m, X_HEADS, X_HEAD_DIM), F32)
    vmem = 2 * m * d * 4 + m * d * 2 + 4 * d * xw * 4 + 2 * d * xw * 2 + 6 * m * xw * 4
    return pl.pallas_call(
        _memkv_body,
        grid=(depth,),
        in_specs=[
            pl.BlockSpec((m, d), lambda l: (0, 0)),
            pl.BlockSpec((None, 1, d), lambda l: (l, 0, 0)),
            pl.BlockSpec((None, d, xw), lambda l: (l, 0, 0)),
            pl.BlockSpec((None, d, xw), lambda l: (l, 0, 0)),
        ],
        out_specs=[pl.BlockSpec((None, m, X_HEADS, X_HEAD_DIM), lambda l: (l, 0, 0, 0))] * 2,
        out_shape=[out, out],
        compiler_params=_params(vmem, 1),
        name="memkv",
    )(mem, norm_mem.reshape(depth, 1, d), w_xk, w_xv)


def _rmsnorm_body(x_ref, g_ref, o_ref):
    o_ref[...] = _rms(x_ref[...], g_ref[...])


def _rmsnorm(x, g, *, tm_cap=512):
    m, d = x.shape
    tm = _row_tile(m, tm_cap)
    return pl.pallas_call(
        _rmsnorm_body,
        grid=(m // tm,),
        in_specs=[pl.BlockSpec((tm, d), lambda i: (i, 0)), pl.BlockSpec((1, d), lambda i: (0, 0))],
        out_specs=pl.BlockSpec((tm, d), lambda i: (i, 0)),
        out_shape=jax.ShapeDtypeStruct((m, d), F32),
        compiler_params=_params(6 * tm * d * 4, 1),
        name="rmsnorm",
    )(x, g.reshape(1, d))


def _suffix_matrix(n):
    r = lax.broadcasted_iota(jnp.int32, (n, n), 0)
    c = lax.broadcasted_iota(jnp.int32, (n, n), 1)
    return jnp.where(r >= c, 1.0, 0.0).astype(BF16)


def _sb_suffix(z, mask, u_mat):
    nz = -z
    lk = jnp.minimum(nz, 0.0) - jnp.log2(1.0 + jnp.exp2(jnp.minimum(z, nz)))
    if mask is not None:
        lk = jnp.where(mask, lk, 0.0)
    hi, lo = _split_hi_lo(lk)
    return _dot(hi, u_mat) + _dot(lo, u_mat)


def _sb_weights(z, incl, mask, acc):
    w = jnp.exp2(z + incl + acc)
    return w if mask is None else jnp.where(mask, w, 0.0)


def _sb_tile(z, mask, acc, u_mat):
    incl = _sb_suffix(z, mask, u_mat)
    return _sb_weights(z, incl, mask, acc), acc + incl[:, 0:1]


def _sbp_body(bias_ref, q_ref, k_ref, v_ref, o_ref, *, tq, tk, scale):
    h = pl.program_id(1)
    i = pl.program_id(2)
    bias = bias_ref[h] * LOG2E
    q = (q_ref[...] * (scale * LOG2E)).astype(BF16)
    u_mat = _suffix_matrix(tk)
    n_diag = tq // tk
    nk = (i + 1) * n_diag

    def logits(q_rows, start):
        kb = k_ref[pl.ds(start, tk), :].astype(BF16)
        return lax.dot_general(q_rows, kb, NT_DIMS, preferred_element_type=F32) + bias

    def sweep(tiles, acc, out):
        incls = [_sb_suffix(z, mask, u_mat) for z, mask, _ in tiles]
        for (z, mask, start), incl in zip(tiles, incls):
            w = _sb_weights(z, incl, mask, acc).astype(BF16)
            acc = acc + incl[:, 0:1]
            out = out + _dot(w, v_ref[pl.ds(start, tk), :].astype(BF16))
        return acc, out

    def diagonal_group():
        base = i * tq
        tri = lax.broadcasted_iota(jnp.int32, (tk, tk), 1) < lax.broadcasted_iota(jnp.int32, (tk, tk), 0)
        starts = [pl.multiple_of(base + c * tk, tk) for c in range(n_diag)]
        zs = {(s, c): logits(q[s * tk:(s + 1) * tk], starts[c]) for s in range(n_diag) for c in range(s + 1)}
        accs, outs = [], []
        for s in range(n_diag):
            tiles = [(zs[s, c], tri if c == s else None, starts[c]) for c in range(s, -1, -1)]
            acc, out = sweep(tiles, jnp.zeros((tk, 1), F32), jnp.zeros((tk, dh), F32))
            accs.append(acc)
            outs.append(out)
        return jnp.concatenate(accs, axis=0), jnp.concatenate(outs, axis=0)

    def unmasked_group(m, carry):
        starts = [pl.multiple_of((nk - 1 - (n_diag * m + r)) * tk, tk) for r in range(n_diag)]
        return sweep([(logits(q, s), None, s) for s in starts], *carry)

    dh = q_ref.shape[1]
    _, out = lax.fori_loop(1, i + 1, unmasked_group, diagonal_group())
    o_ref[...] = out.astype(o_ref.dtype)


def _sb_attend_prompt(qkv, bias, nb, t, *, tq=2048):
    m, d3 = qkv.shape
    d = d3 // 3
    dh = d // SB_HEADS
    tq = min(tq, t)
    tk = min(SB_KEY_TILE, tq)
    nq = t // tq
    vmem = 4 * t * dh * 4 + 4 * tq * dh * 4 + 16 * tq * tk * 4
    return pl.pallas_call(
        functools.partial(_sbp_body, tq=tq, tk=tk, scale=dh ** -0.5),
        grid=(nb, SB_HEADS, nq),
        in_specs=[
            pl.BlockSpec(memory_space=pltpu.SMEM),
            pl.BlockSpec((tq, dh), lambda b, h, i: (b * nq + i, h)),
            pl.BlockSpec((t, dh), lambda b, h, i: (b, SB_HEADS + h)),
            pl.BlockSpec((t, dh), lambda b, h, i: (b, 2 * SB_HEADS + h)),
        ],
        out_specs=pl.BlockSpec((tq, dh), lambda b, h, i: (b * nq + i, h)),
        out_shape=jax.ShapeDtypeStruct((m, d), _mxu_feed_dtype(tq)),
        compiler_params=_params(vmem, 3),
        name="sb_prompt",
    )(bias, qkv, qkv, qkv)


def _sbs_body(pt_ref, bias_ref, qkv_ref, *refs, t, dh, page, npp, scale):
    del pt_ref
    k_refs, v_refs = refs[:npp], refs[npp:2 * npp]
    o_ref, out_s, acc_s = refs[2 * npp:]
    s = pl.program_id(1)
    d = SB_HEADS * dh
    rows = SB_HEADS * t
    u_mat = _suffix_matrix(page)

    def attend(pages, mask, acc, outs):
        qs = [(qkv_ref[:, h * dh:(h + 1) * dh] * (scale * LOG2E)).astype(BF16) for h in range(SB_HEADS)]
        zs = [jnp.concatenate([lax.dot_general(qs[h], get_k(h), NT_DIMS, preferred_element_type=F32)
                               + bias_ref[h] * LOG2E for h in range(SB_HEADS)], axis=0)
              for get_k, _ in pages]
        incls = [_sb_suffix(z, mask, u_mat) for z in zs]
        ws = []
        for z, incl in zip(zs, incls):
            ws.append(_sb_weights(z, incl, mask, acc))
            acc = acc + incl[:, 0:1]
        for w, (_, get_v) in zip(ws, pages):
            outs = [outs[h] + _dot(w[h * t:(h + 1) * t, :].astype(BF16), get_v(h)) for h in range(SB_HEADS)]
        return acc, outs

    def store(acc, outs):
        acc_s[...] = acc
        for h in range(SB_HEADS):
            out_s[h * t:(h + 1) * t, :] = outs[h]

    @pl.when(s == 0)
    def _():
        pad = jnp.zeros((page - t, dh), F32)
        key = lax.broadcasted_iota(jnp.int32, (rows, page), 1)
        qry = lax.broadcasted_iota(jnp.int32, (rows, page), 0) % t
        new_tokens = (
            lambda h: jnp.concatenate([qkv_ref[:, d + h * dh:d + (h + 1) * dh], pad], axis=0).astype(BF16),
            lambda h: jnp.concatenate([qkv_ref[:, 2 * d + h * dh:2 * d + (h + 1) * dh], pad], axis=0).astype(BF16))
        acc, outs = attend([new_tokens], key < qry, jnp.zeros((rows, 1), F32),
                           [jnp.zeros((t, dh), F32)] * SB_HEADS)
        store(acc, outs)

    def cached_page(j):
        return (lambda h: k_refs[j][pl.ds(h, page, stride=SB_HEADS), :].astype(BF16),
                lambda h: v_refs[j][pl.ds(h, page, stride=SB_HEADS), :].astype(BF16))

    acc, outs = attend([cached_page(j) for j in range(npp)], None, acc_s[...],
                       [out_s[h * t:(h + 1) * t, :] for h in range(SB_HEADS)])
    store(acc, outs)

    @pl.when(s == pl.num_programs(1) - 1)
    def _():
        for h in range(SB_HEADS):
            o_ref[:, h * dh:(h + 1) * dh] = outs[h]


def _sb_attend_sample(qkv, bias, cache_k, cache_v, page_table, nb, t, *, npp=8):
    m, d3 = qkv.shape
    d = d3 // 3
    dh = d // SB_HEADS
    n_phys, page = cache_k.shape[0], cache_k.shape[1]
    n_pages = page_table.shape[1]
    while n_pages % npp:
        npp //= 2
    ck = cache_k.reshape(n_phys, page * SB_HEADS, dh)
    cv = cache_v.reshape(n_phys, page * SB_HEADS, dh)

    def cache_spec(j):
        return pl.BlockSpec((None, page * SB_HEADS, dh),
                            lambda b, s, pt: (pt[b, n_pages - 1 - (s * npp + j)], 0, 0))

    vmem = 4 * npp * page * d * 4 + 2 * t * d3 * 4 + 24 * npp * SB_HEADS * t * page * 4
    return pl.pallas_call(
        functools.partial(_sbs_body, t=t, dh=dh, page=page, npp=npp, scale=dh ** -0.5),
        grid_spec=pltpu.PrefetchScalarGridSpec(
            num_scalar_prefetch=1,
            grid=(nb, n_pages // npp),
            in_specs=[pl.BlockSpec(memory_space=pltpu.SMEM), pl.BlockSpec((t, d3), lambda b, s, pt: (b, 0))]
            + [cache_spec(j) for j in range(npp)] * 2,
            out_specs=pl.BlockSpec((t, d), lambda b, s, pt: (b, 0)),
            scratch_shapes=[pltpu.VMEM((SB_HEADS * t, dh), F32), pltpu.VMEM((SB_HEADS * t, 1), F32)],
        ),
        out_shape=jax.ShapeDtypeStruct((m, d), F32),
        compiler_params=_params(vmem, 2),
        name="sb_sample",
    )(page_table, bias, qkv, *([ck] * npp), *([cv] * npp))


def _cmul(ar, ai, br, bi):
    return ar * br - ai * bi, ar * bi + ai * br


def _gelu_tanh(x):
    return 0.5 * x * (1.0 + jnp.tanh(math.sqrt(2.0 / math.pi) * (x + 0.044715 * (x * x * x))))


def _s5_body(h_ref, ar_ref, ai_ref, br_ref, bi_ref, cr_ref, ci_ref, d_ref, s0r_ref, s0i_ref,
             y_ref, sr_ref, si_ref, xr_s, xi_s, pr_s, pi_s, cr_s, ci_s, *, nbk):
    t = pl.program_id(2)
    rows8, n_state = pr_s.shape
    row = lax.broadcasted_iota(jnp.int32, (rows8, n_state), 0)

    def last_step_to_all_rows(x):
        x = jnp.where(row >= rows8 - nbk, x, 0.0)
        have = nbk
        while have < rows8:
            x = x + pltpu.roll(x, rows8 - have, 0)
            have *= 2
        return x

    @pl.when(t == 0)
    def _():
        ar, ai = ar_ref[...], ai_ref[...]
        pr, pi = ar, ai
        for r in range(rows8):
            if r and r % nbk == 0:
                pr, pi = _cmul(pr, pi, ar, ai)
            pr_s[r:r + 1, :] = pr
            pi_s[r:r + 1, :] = pi
        cr_s[...] = jnp.zeros_like(cr_s)
        ci_s[...] = jnp.zeros_like(ci_s)
        cr_s[rows8 - nbk:, :] = s0r_ref[...]
        ci_s[rows8 - nbk:, :] = s0i_ref[...]
        cr_s[...] = last_step_to_all_rows(cr_s[...])
        ci_s[...] = last_step_to_all_rows(ci_s[...])

    tt = h_ref.shape[1]
    n_slab = xr_s.shape[0]
    lanes = [slice(s * V7X_LANES, (s + 1) * V7X_LANES) for s in range(n_slab)]

    def batch_rows(b):
        return pl.ds(b, tt, stride=nbk) if nbk > 1 else pl.ds(0, tt)

    b_r, b_i = br_ref[...].astype(BF16), bi_ref[...].astype(BF16)
    for b in range(nbk):
        u = h_ref[b].astype(BF16)
        bu_r, bu_i = _dot(u, b_r), _dot(u, b_i)
        for s in range(n_slab):
            xr_s[s, batch_rows(b), :] = bu_r[:, lanes[s]]
            xi_s[s, batch_rows(b), :] = bu_i[:, lanes[s]]
    pw_r, pw_i = pr_s[...], pi_s[...]

    def tile(blk, carry):
        c_r, c_i = carry
        rows = pl.ds(pl.multiple_of(blk * rows8, rows8), rows8)
        x_r = jnp.concatenate([xr_s[s, rows, :] for s in range(n_slab)], axis=1)
        x_i = jnp.concatenate([xi_s[s, rows, :] for s in range(n_slab)], axis=1)
        dist = 1
        while dist * nbk < rows8:
            lo = (dist - 1) * nbk
            a_r, a_i = pw_r[lo:lo + 1, :], pw_i[lo:lo + 1, :]
            keep = row >= dist * nbk
            sh_r = jnp.where(keep, pltpu.roll(x_r, dist * nbk, 0), 0.0)
            sh_i = jnp.where(keep, pltpu.roll(x_i, dist * nbk, 0), 0.0)
            d_r, d_i = _cmul(a_r, a_i, sh_r, sh_i)
            x_r, x_i = x_r + d_r, x_i + d_i
            dist *= 2
        d_r, d_i = _cmul(pw_r, pw_i, c_r, c_i)
        x_r, x_i = x_r + d_r, x_i + d_i
        for s in range(n_slab):
            xr_s[s, rows, :] = x_r[:, lanes[s]]
            xi_s[s, rows, :] = x_i[:, lanes[s]]
        return last_step_to_all_rows(x_r), last_step_to_all_rows(x_i)

    c_r, c_i = lax.fori_loop(0, xr_s.shape[1] // rows8, tile, (cr_s[...], ci_s[...]))
    cr_s[...] = c_r
    ci_s[...] = c_i

    c_re, c_im = cr_ref[...].astype(BF16), ci_ref[...].astype(BF16)
    for b in range(nbk):
        st_r = jnp.concatenate([xr_s[s, batch_rows(b), :] for s in range(n_slab)], axis=1).astype(BF16)
        st_i = jnp.concatenate([xi_s[s, batch_rows(b), :] for s in range(n_slab)], axis=1).astype(BF16)
        y = _dot(st_r, c_re) - _dot(st_i, c_im)
        y_ref[b] = _gelu_tanh(y + d_ref[...] * h_ref[b]).astype(y_ref.dtype)

    @pl.when(t == pl.num_programs(2) - 1)
    def _():
        sr_ref[...] = c_r[rows8 - nbk:, :]
        si_ref[...] = c_i[rows8 - nbk:, :]


def _s5_discretize(a_re, a_im, log_dt, b_re, b_im, c_re, c_im):
    g, p = a_re.shape
    gb = S5_CH_BLOCK // S5_GROUP_CH
    nblk = g // gb
    dt = jnp.exp(log_dt)[:, None]
    mag = jnp.exp(a_re * dt)
    abar_r, abar_i = mag * jnp.cos(a_im * dt), mag * jnp.sin(a_im * dt)
    xr, xi = abar_r - 1.0, abar_i
    den = a_re * a_re + a_im * a_im
    coef_r = (xr * a_re + xi * a_im) / den
    coef_i = (xi * a_re - xr * a_im) / den
    bbar_r = coef_r[..., None] * b_re - coef_i[..., None] * b_im
    bbar_i = coef_r[..., None] * b_im + coef_i[..., None] * b_re
    eye = jnp.eye(gb, dtype=F32)

    def b_big(bb):
        blk = bb.reshape(nblk, gb, p, S5_GROUP_CH).transpose(0, 1, 3, 2)
        return blk[:, :, :, None, :] * eye[None, :, None, :, None]

    def c_big(cc):
        blk = cc.reshape(nblk, gb, S5_GROUP_CH, p).transpose(0, 1, 3, 2)
        return blk[:, :, :, None, :] * eye[None, :, None, :, None]

    n_state = gb * p
    return (abar_r.reshape(1, g * p), abar_i.reshape(1, g * p),
            b_big(bbar_r).reshape(nblk, S5_CH_BLOCK, n_state), b_big(bbar_i).reshape(nblk, S5_CH_BLOCK, n_state),
            c_big(c_re).reshape(nblk, n_state, S5_CH_BLOCK), c_big(c_im).reshape(nblk, n_state, S5_CH_BLOCK))


def _s5_core(h, disc, d_skip, s0r, s0i, nb, t, *, tt_cap=256):
    m, d = h.shape
    nbk = math.gcd(nb, V7X_SUBLANES)
    abar_r, abar_i, bbr, bbi, ccr, cci = disc
    nblk, cb, n_state = bbr.shape
    tt = _row_tile(t, tt_cap)
    nt = t // tt
    ng = nb // nbk
    state = jax.ShapeDtypeStruct((ng, nbk, nblk * n_state), F32)
    lane_spec = pl.BlockSpec((1, n_state), lambda b, c, i: (0, c))
    state_spec = pl.BlockSpec((None, nbk, n_state), lambda b, c, i: (b, 0, c))
    b_spec = pl.BlockSpec((None, cb, n_state), lambda b, c, i: (c, 0, 0))
    c_spec = pl.BlockSpec((None, n_state, cb), lambda b, c, i: (c, 0, 0))
    row_spec = pl.BlockSpec((nbk, tt, cb), lambda b, c, i: (b, i, c))
    vmem = 4 * tt * nbk * cb * 4 + 16 * cb * n_state * 4 + 8 * tt * nbk * n_state * 4
    lane_tile = pltpu.VMEM((V7X_SUBLANES, n_state), F32)
    slabs = pltpu.VMEM((n_state // V7X_LANES, tt * nbk, V7X_LANES), F32)
    y, sr, si = pl.pallas_call(
        functools.partial(_s5_body, nbk=nbk),
        grid=(ng, nblk, nt),
        in_specs=[row_spec, lane_spec, lane_spec, b_spec, b_spec, c_spec, c_spec,
                  pl.BlockSpec((1, cb), lambda b, c, i: (0, c)), state_spec, state_spec],
        out_specs=[row_spec, state_spec, state_spec],
        out_shape=[jax.ShapeDtypeStruct((nb, t, d), _mxu_feed_dtype(tt)), state, state],
        scratch_shapes=[slabs, slabs, lane_tile, lane_tile, lane_tile, lane_tile],
        compiler_params=_params(vmem, 3),
        name="s5",
    )(h.reshape(nb, t, d), abar_r, abar_i, bbr, bbi, ccr, cci, d_skip.reshape(1, d),
      s0r.reshape(ng, nbk, -1), s0i.reshape(ng, nbk, -1))
    return y.reshape(m, d), sr.reshape(nb, -1), si.reshape(nb, -1)


def _conv_body(b_ref, c_ref, v_ref, w_ref, p_ref, y_ref, s_ref):
    z = c_ref[...] * v_ref[...]
    t = z.shape[0]
    row = lax.broadcasted_iota(jnp.int32, z.shape, 0)
    p0, p1 = p_ref[0:1, :], p_ref[1:2, :]
    z1 = jnp.where(row >= 1, pltpu.roll(z, 1, 0), p1)
    z2 = jnp.where(row >= 2, pltpu.roll(z, 2, 0), jnp.where(row == 0, p0, p1))
    y = w_ref[0:1, :] * z2 + w_ref[1:2, :] * z1 + w_ref[2:3, :] * z
    y_ref[...] = (b_ref[...] * y).astype(y_ref.dtype)
    s_ref[...] = z[t - (CONV_W - 1):, :]


def _conv_core(bcv, w_conv, prev, nb, t, *, tc=512):
    m, d3 = bcv.shape
    d = d3 // 3
    nc = d // tc
    vmem = 12 * t * tc * 4
    return pl.pallas_call(
        _conv_body,
        grid=(nb, nc),
        in_specs=[
            pl.BlockSpec((t, tc), lambda b, j: (b, j)),
            pl.BlockSpec((t, tc), lambda b, j: (b, nc + j)),
            pl.BlockSpec((t, tc), lambda b, j: (b, 2 * nc + j)),
            pl.BlockSpec((CONV_W, tc), lambda b, j: (0, j)),
            pl.BlockSpec((None, CONV_W - 1, tc), lambda b, j: (b, 0, j)),
        ],
        out_specs=[pl.BlockSpec((t, tc), lambda b, j: (b, j)),
                   pl.BlockSpec((None, CONV_W - 1, tc), lambda b, j: (b, 0, j))],
        out_shape=[jax.ShapeDtypeStruct((m, d), _mxu_feed_dtype(t)), jax.ShapeDtypeStruct((nb, CONV_W - 1, d), F32)],
        compiler_params=_params(vmem, 2),
        name="conv",
    )(bcv, bcv, bcv, w_conv, prev)


def _gla_gate_body(x_ref, g_ref, w1_ref, w2_ref, b_ref, o_ref):
    h = _rms(x_ref[...], g_ref[...]).astype(BF16)
    low = _dot(h, w1_ref[...].astype(BF16))
    y = _dot(low.astype(BF16), w2_ref[...].astype(BF16)) + b_ref[...]
    o_ref[...] = -_softplus(-y) / GLA_TAU


def _gla_gate(x, g, w1, w2, b, *, tm_cap=512):
    m, d = x.shape
    r = w1.shape[1]
    gk = w2.shape[1]
    rp = V7X_LANES
    w1p = jnp.pad(w1, ((0, 0), (0, rp - r)))
    w2p = jnp.pad(w2, ((0, rp - r), (0, 0)))
    tm = _row_tile(m, tm_cap)
    vmem = 3 * tm * d * 4 + 4 * d * rp * 4 + 4 * rp * gk * 4 + 6 * tm * gk * 4
    return pl.pallas_call(
        _gla_gate_body,
        grid=(m // tm,),
        in_specs=[
            pl.BlockSpec((tm, d), lambda i: (i, 0)),
            pl.BlockSpec((1, d), lambda i: (0, 0)),
            pl.BlockSpec((d, rp), lambda i: (0, 0)),
            pl.BlockSpec((rp, gk), lambda i: (0, 0)),
            pl.BlockSpec((1, gk), lambda i: (0, 0)),
        ],
        out_specs=pl.BlockSpec((tm, gk), lambda i: (i, 0)),
        out_shape=jax.ShapeDtypeStruct((m, gk), F32),
        compiler_params=_params(vmem, 1),
        name="gla_gate",
    )(x, g.reshape(1, d), w1p, w2p, b.reshape(1, gk))


def _cumsum_rows(x):
    n = x.shape[0]
    row = lax.broadcasted_iota(jnp.int32, x.shape, 0)
    dist = 1
    while dist < n:
        x = x + jnp.where(row >= dist, pltpu.roll(x, dist, 0), 0.0)
        dist *= 2
    return x


def _gla_chunk(q, k, v, glog, st):
    c, dk = q.shape
    b = _cumsum_rows(glog)
    o = lax.dot_general((q * jnp.exp(b)).astype(BF16), st.astype(BF16), NT_DIMS, preferred_element_type=F32)
    rowc = lax.broadcasted_iota(jnp.int32, (c, 1), 0)
    lane = lax.broadcasted_iota(jnp.int32, (1, c), 1)
    rsub = lax.broadcasted_iota(jnp.int32, (GLA_SUB, 1), 0)
    att_rows = []
    for blk in range(c // GLA_SUB):
        lo = blk * GLA_SUB
        b_i = b[lo:lo + GLA_SUB]
        q_i = q[lo:lo + GLA_SUB]
        if blk > 0:
            b_0 = b[lo - 1:lo]
            q_off = (q_i * jnp.exp(b_i - b_0)).astype(BF16)
            k_off = (k * jnp.exp(jnp.where(rowc < lo, b_0 - b, NEG_INF))).astype(BF16)
            att = lax.dot_general(q_off, k_off, NT_DIMS, preferred_element_type=F32)
        else:
            att = jnp.zeros((GLA_SUB, c), F32)
        for jj in range(GLA_SUB):
            j = lo + jj
            dec = jnp.exp(jnp.where(rsub >= jj, b_i - b[j:j + 1], NEG_INF))
            col = jnp.sum(q_i * k[j:j + 1] * dec, axis=1, keepdims=True)
            att = att + jnp.where(lane == j, col, 0.0)
        att_rows.append(att)
    att = att_rows[0] if len(att_rows) == 1 else jnp.concatenate(att_rows, axis=0)
    vb = v.astype(BF16)
    o = o + _dot(att.astype(BF16), vb)
    b_last = b[c - 1:c]
    k_dec = (k * jnp.exp(b_last - b)).astype(BF16)
    st = st * jnp.exp(b_last) + lax.dot_general(vb, k_dec, TN_DIMS, preferred_element_type=F32)
    return o, st


def _gla_body(*refs, chunk, has_prev, scale):
    refs = list(refs)
    q_ref, k_ref, v_ref, r_ref, gl_ref, ng_ref = refs[:6]
    s0_ref = refs[6] if has_prev else None
    o_ref, s_ref, st_s = refs[-3:]
    t = pl.program_id(1)
    n_heads, dv, dk = st_s.shape

    @pl.when(t == 0)
    def _():
        for h in range(n_heads):
            st_s[h] = s0_ref[h].T if has_prev else jnp.zeros((dv, dk), F32)

    def step(ci, carry):
        rows = pl.ds(pl.multiple_of(ci * chunk, chunk), chunk)
        for h in range(n_heads):
            ck, cv = slice(h * dk, (h + 1) * dk), slice(h * dv, (h + 1) * dv)
            o, st = _gla_chunk(q_ref[rows, ck] * scale, k_ref[rows, ck], v_ref[rows, cv], gl_ref[rows, ck], st_s[h])
            st_s[h] = st
            o = o * lax.rsqrt(jnp.mean(o * o, axis=-1, keepdims=True) + EPS) * ng_ref[...]
            o_ref[rows, cv] = (o * jax.nn.silu(r_ref[rows, cv])).astype(o_ref.dtype)
        return carry

    lax.fori_loop(0, q_ref.shape[0] // chunk, step, 0)

    @pl.when(t == pl.num_programs(1) - 1)
    def _():
        for h in range(n_heads):
            s_ref[h] = st_s[h].T


def _gla_core(qkvr, glog, norm_g, prev, nb, t, *, tt_cap=256):
    m = qkvr.shape[0]
    gk = glog.shape[1]
    dk = gk // GLA_HEADS
    d = (qkvr.shape[1] - 2 * gk) // 2
    dv = d // GLA_HEADS
    chunk = GLA_CHUNK if t % GLA_CHUNK == 0 else t
    tt = _row_tile(t, tt_cap)
    nt = t // tt
    in_specs = [
        pl.BlockSpec((tt, gk), lambda b, i: (b * nt + i, 0)),
        pl.BlockSpec((tt, gk), lambda b, i: (b * nt + i, 1)),
        pl.BlockSpec((tt, d), lambda b, i: (b * nt + i, 2 * gk // d)),
        pl.BlockSpec((tt, d), lambda b, i: (b * nt + i, 2 * gk // d + 1)),
        pl.BlockSpec((tt, gk), lambda b, i: (b * nt + i, 0)),
        pl.BlockSpec((1, dv), lambda b, i: (0, 0)),
    ]
    args = [qkvr, qkvr, qkvr, qkvr, glog, norm_g.reshape(1, dv)]
    state_spec = pl.BlockSpec((None, GLA_HEADS, dk, dv), lambda b, i: (b, 0, 0, 0))
    if prev is not None:
        in_specs.append(state_spec)
        args.append(prev)
    vmem = 4 * tt * (3 * gk + 3 * d) * 4 + 7 * GLA_HEADS * dk * dv * 4 + 64 * GLA_HEADS * chunk * dv * 4
    return pl.pallas_call(
        functools.partial(_gla_body, chunk=chunk, has_prev=prev is not None, scale=dk ** -0.5),
        grid=(nb, nt),
        in_specs=in_specs,
        out_specs=[pl.BlockSpec((tt, d), lambda b, i: (b * nt + i, 0)), state_spec],
        out_shape=[jax.ShapeDtypeStruct((m, d), _mxu_feed_dtype(chunk)),
                   jax.ShapeDtypeStruct((nb, GLA_HEADS, dk, dv), F32)],
        scratch_shapes=[pltpu.VMEM((GLA_HEADS, dv, dk), F32)],
        compiler_params=_params(vmem, 2),
        name="gla",
    )(*args)


def _trunk(xp, xs, prompt_dims, sample_dims, mem_p, mem_s, p, s5_disc, sample):
    (nbp, tp), (nbs, ts) = prompt_dims, sample_dims
    d = xp.shape[1]
    depth = p["norm_mix"].shape[0]
    new_p, new_s = {}, {}
    for i in range(depth):
        g_mix = p["norm_mix"][i]
        kind = i % 4
        if kind == 0:
            qkv_p, qkv_s = _mm(xp, p["w_sb_qkv"], g=g_mix, small=xs)
            o_p = _sb_attend_prompt(qkv_p, p["sb_bias"], nbp, tp)
            o_s = _sb_attend_sample(qkv_s, p["sb_bias"], sample["cache_sb_k"], sample["cache_sb_v"],
                                    sample["page_table"], nbs, ts)
            xp, xs = _mm(o_p, p["w_sb_out"], res=xp, small=o_s, small_res=xs)
            for new, qkv, nb, t in ((new_p, qkv_p, nbp, tp), (new_s, qkv_s, nbs, ts)):
                shp = (nb, t, SB_HEADS, d // SB_HEADS)
                new["sb"] = (qkv[:, d:2 * d].reshape(shp), qkv[:, 2 * d:].reshape(shp))
        elif kind == 1:
            n_state = d // S5_GROUP_CH * S5_STATE
            zeros = jnp.zeros((nbp, n_state), F32)
            y_p, sr_p, si_p = _s5_core(_rmsnorm(xp, g_mix), s5_disc, p["s5_d"], zeros, zeros, nbp, tp)
            y_s, sr_s, si_s = _s5_core(_rmsnorm(xs, g_mix), s5_disc, p["s5_d"], sample["state_s5_re"],
                                       sample["state_s5_im"], nbs, ts)
            xp, xs = _mm(y_p, p["w_s5_glu"], res=xp, glu=True, small=y_s, small_res=xs)
            for new, sr, si, nb in ((new_p, sr_p, si_p, nbp), (new_s, sr_s, si_s, nbs)):
                sshape = (nb, d // S5_GROUP_CH, S5_STATE)
                new["s5"] = (sr.reshape(sshape), si.reshape(sshape))
        elif kind == 2:
            bcv_p, bcv_s = _mm(xp, p["w_conv_in"], g=g_mix, small=xs)
            yg_p, new_p["conv"] = _conv_core(bcv_p, p["w_conv"], jnp.zeros((nbp, CONV_W - 1, d), F32), nbp, tp)
            yg_s, new_s["conv"] = _conv_core(bcv_s, p["w_conv"], sample["state_conv"], nbs, ts)
            xp, xs = _mm(yg_p, p["w_conv_out"], res=xp, small=yg_s, small_res=xs)
        else:
            qkvr_p, qkvr_s = _mm(xp, p["w_gla_in"], g=g_mix, small=xs)
            gate = (g_mix, p["w_gla_g1"], p["w_gla_g2"], p["b_gla_g"])
            o_p, new_p["gla"] = _gla_core(qkvr_p, _gla_gate(xp, *gate), p["gla_norm"], None, nbp, tp)
            o_s, new_s["gla"] = _gla_core(qkvr_s, _gla_gate(xs, *gate), p["gla_norm"], sample["state_gla"], nbs, ts)
            xp, xs = _mm(o_p, p["w_gla_out"], res=xp, small=o_s, small_res=xs)
        xp = _xattn(xp, p["norm_xattn"][i], p["w_xq"], p["w_xo"], mem_p[0], mem_p[1], i, tp)
        xs = _xattn(xs, p["norm_xattn"][i], p["w_xq"], p["w_xo"], mem_s[0], mem_s[1], i, ts)
        xp, xs = _ffn(xp, p["norm_ffn"][i], p["w_ffn_in"], p["w_ffn_out"], i, small=xs,
                      final_g=p["norm_final"] if i == depth - 1 else None)
    return xp.reshape(nbp, tp, d), xs.reshape(nbs, ts, d), new_p, new_s


def kernel(x_prompt, x_sample, mem_prompt, cache_sb_k, cache_sb_v, page_table, state_s5_re, state_s5_im, state_conv, state_gla, cache_mem_k, cache_mem_v, norm_mix, norm_xattn, norm_mem, norm_ffn, norm_final, w_sb_qkv, w_sb_out, sb_bias, s5_a_re, s5_a_im, s5_log_dt, s5_b_re, s5_b_im, s5_c_re, s5_c_im, s5_d, w_s5_glu, w_conv_in, w_conv, w_conv_out, w_gla_in, w_gla_g1, w_gla_g2, b_gla_g, gla_norm, w_gla_out, w_xq, w_xk, w_xv, w_xo, w_ffn_in, w_ffn_out):
    p = dict(norm_mix=norm_mix, norm_xattn=norm_xattn, norm_ffn=norm_ffn, norm_final=norm_final,
             w_sb_qkv=w_sb_qkv, w_sb_out=w_sb_out, sb_bias=sb_bias, s5_d=s5_d, w_s5_glu=w_s5_glu,
             w_conv_in=w_conv_in, w_conv=w_conv, w_conv_out=w_conv_out,
             w_gla_in=w_gla_in, w_gla_g1=w_gla_g1, w_gla_g2=w_gla_g2, b_gla_g=b_gla_g,
             gla_norm=gla_norm, w_gla_out=w_gla_out, w_xq=w_xq, w_xo=w_xo,
             w_ffn_in=w_ffn_in, w_ffn_out=w_ffn_out)
    nbp, tp, d = x_prompt.shape
    nbs, ts, _ = x_sample.shape
    depth = norm_mix.shape[0]
    n_mem = mem_prompt.shape[1]
    mshape = (depth, nbp, n_mem, X_HEADS, X_HEAD_DIM)
    mem_k_p, mem_v_p = _memkv(mem_prompt.reshape(nbp * n_mem, d), norm_mem, w_xk, w_xv)
    mem_k_p, mem_v_p = mem_k_p.reshape(mshape), mem_v_p.reshape(mshape)
    s5_disc = _s5_discretize(s5_a_re, s5_a_im, s5_log_dt, s5_b_re, s5_b_im, s5_c_re, s5_c_im)

    sample = dict(cache_sb_k=cache_sb_k, cache_sb_v=cache_sb_v, page_table=page_table,
                  state_s5_re=state_s5_re.reshape(nbs, -1), state_s5_im=state_s5_im.reshape(nbs, -1),
                  state_conv=state_conv, state_gla=state_gla)
    y_p, y_s, new_p, new_s = _trunk(x_prompt.reshape(nbp * tp, d), x_sample.reshape(nbs * ts, d),
                                    (nbp, tp), (nbs, ts), (mem_k_p, mem_v_p), (cache_mem_k, cache_mem_v),
                                    p, s5_disc, sample)

    return (y_p, y_s, new_p["sb"][0], new_p["sb"][1], new_s["sb"][0], new_s["sb"][1],
            new_p["s5"][0], new_p["s5"][1], new_s["s5"][0], new_s["s5"][1],
            new_p["conv"], new_s["conv"], new_p["gla"], new_s["gla"],
            mem_k_p, mem_v_p)
```

```python
import functools
import math

import jax
import jax.numpy as jnp
from jax import lax
from jax.experimental import pallas as pl
from jax.experimental.pallas import tpu as pltpu

F32 = jnp.float32
BF16 = jnp.bfloat16
EPS = 1e-6
NEG_INF = float("-inf")
LOG2E = 1.4426950408889634

V7X_VMEM_BYTES = 64 * 1024 * 1024
V7X_SUBLANES = 8
V7X_LANES = 128

SB_HEADS = 16
SB_KEY_TILE = 256
S5_GROUP_CH = 16
S5_STATE = 64
S5_CH_BLOCK = 256
CONV_W = 3
GLA_HEADS = 4
GLA_TAU = 16.0
GLA_CHUNK = 32
GLA_SUB = 8
X_HEADS = 4
X_HEAD_DIM = 128

NT_DIMS = (((1,), (1,)), ((), ()))
TN_DIMS = (((0,), (0,)), ((), ()))


def _params(vmem_bytes, n_grid):
    limit = int(min(V7X_VMEM_BYTES - (6 << 20), max(vmem_bytes * 5 // 4 + (4 << 20), 16 << 20)))
    return pltpu.CompilerParams(dimension_semantics=("arbitrary",) * n_grid, vmem_limit_bytes=limit)


def _rms(x, g):
    return x * lax.rsqrt(jnp.mean(x * x, axis=-1, keepdims=True) + EPS) * g


def _softplus(z):
    return jnp.maximum(z, 0.0) + jnp.log1p(jnp.exp(-jnp.abs(z)))


def _dot(a, b):
    return jnp.dot(a, b, preferred_element_type=F32)


def _split_hi_lo(x):
    hi = x.astype(BF16)
    lo = (x - hi.astype(F32)).astype(BF16)
    return hi, lo


def _mxu_feed_dtype(rows_per_store):
    return BF16 if rows_per_store % (2 * V7X_SUBLANES) == 0 else F32


def _row_tile(m, cap):
    t = min(m, cap)
    while m % t:
        t //= 2
    return t


def _mm_body(*refs, norm, mode, dual):
    refs = list(refs)
    x_ref = refs.pop(0)
    g_ref = refs.pop(0) if norm else None
    w_ref = refs.pop(0)
    w2_ref = refs.pop(0) if mode == "glu" else None
    has_res = mode in ("res", "glu")
    res_ref = refs.pop(0) if has_res else None
    xs_ref = refs.pop(0) if dual else None
    ress_ref = refs.pop(0) if dual and has_res else None
    o_ref = refs.pop(0)
    os_ref = refs.pop(0) if dual else None
    lhs = refs.pop(0)
    tm = x_ref.shape[0]
    first_col = pl.program_id(1) == 0
    first_tile = pl.program_id(0) == 0

    def load_lhs(src, rows):
        x = src[...]
        if norm:
            x = _rms(x, g_ref[...])
        lhs[rows, :] = x.astype(BF16)

    def emit(n_rows):
        a = lhs[0:n_rows, :]
        y = _dot(a, w_ref[...].astype(BF16))
        if mode == "glu":
            y = y * jax.nn.sigmoid(_dot(a, w2_ref[...].astype(BF16)))
        o_ref[...] = res_ref[...] + y[0:tm] if has_res else y[0:tm]
        if n_rows > tm:
            os_ref[...] = ress_ref[...] + y[tm:n_rows] if has_res else y[tm:n_rows]

    pl.when(first_col)(lambda: load_lhs(x_ref, slice(0, tm)))
    if dual:
        n_all = lhs.shape[0]
        pl.when(jnp.logical_and(first_col, first_tile))(lambda: load_lhs(xs_ref, slice(tm, n_all)))
        pl.when(first_tile)(lambda: emit(n_all))
        pl.when(jnp.logical_not(first_tile))(lambda: emit(tm))
    else:
        emit(tm)


def _mm(x, w, *, g=None, res=None, glu=False, small=None, small_res=None, tm_cap=1024):
    m, k = x.shape
    n = w.shape[1] // (2 if glu else 1)
    mode = "glu" if glu else ("res" if res is not None else "plain")
    tn = 1024 if (mode == "plain" and n % 1024 == 0) else 512
    nj = n // tn
    dual = small is not None
    tm = _row_tile(m, tm_cap)
    in_specs = [pl.BlockSpec((tm, k), lambda i, j: (i, 0))]
    args = [x]
    if g is not None:
        in_specs.append(pl.BlockSpec((1, k), lambda i, j: (0, 0)))
        args.append(g.reshape(1, k))
    in_specs.append(pl.BlockSpec((k, tn), lambda i, j: (0, j)))
    args.append(w)
    if glu:
        in_specs.append(pl.BlockSpec((k, tn), lambda i, j: (0, j + nj)))
        args.append(w)
    if res is not None:
        in_specs.append(pl.BlockSpec((tm, tn), lambda i, j: (i, j)))
        args.append(res)
    out_specs = [pl.BlockSpec((tm, tn), lambda i, j: (i, j))]
    out_shape = [jax.ShapeDtypeStruct((m, n), F32)]
    ms = 0
    if dual:
        ms = small.shape[0]
        small_cols = pl.BlockSpec((ms, tn), lambda i, j: (0, jnp.where(i == 0, j, nj - 1)))
        in_specs.append(pl.BlockSpec((ms, k), lambda i, j: (0, 0)))
        args.append(small)
        if res is not None:
            in_specs.append(small_cols)
            args.append(small_res)
        out_specs.append(small_cols)
        out_shape.append(jax.ShapeDtypeStruct((ms, n), F32))
    vmem = (2 * (tm + ms) * k * 4 + (tm + ms) * k * 2 + (2 if glu else 1) * 2 * k * tn * 4
            + 4 * (tm + ms) * tn * 4 + 2 * k * tn * 4)
    outs = pl.pallas_call(
        functools.partial(_mm_body, norm=g is not None, mode=mode, dual=dual),
        grid=(m // tm, nj),
        in_specs=in_specs,
        out_specs=out_specs,
        out_shape=out_shape,
        scratch_shapes=[pltpu.VMEM((tm + ms, k), BF16)],
        compiler_params=_params(vmem, 2),
        name="mm_" + mode,
    )(*args)
    return tuple(outs) if dual else outs[0]


def _ffn_body(*refs, dual, final_norm):
    refs = list(refs)
    x_ref, g_ref, wg_ref, wu_ref, wo_ref = refs[:5]
    refs = refs[5:]
    gf_ref = refs.pop(0) if final_norm else None
    xs_ref = refs.pop(0) if dual else None
    o_ref = refs.pop(0)
    os_ref = refs.pop(0) if dual else None
    h_scr = refs.pop(0)
    tm = x_ref.shape[0]
    first_col = pl.program_id(1) == 0
    last_col = pl.program_id(1) == pl.num_programs(1) - 1
    first_tile = pl.program_id(0) == 0

    def start(src, rows, out):
        x = src[...]
        h_scr[rows, :] = _rms(x, g_ref[...]).astype(BF16)
        out[...] = x

    def accumulate(n_rows):
        h = h_scr[0:n_rows, :]
        gate = _dot(h, wg_ref[...].astype(BF16))
        up = _dot(h, wu_ref[...].astype(BF16))
        a = (jax.nn.silu(gate) * up).astype(BF16)
        y = _dot(a, wo_ref[...].astype(BF16))
        o_ref[...] += y[0:tm]
        if n_rows > tm:
            os_ref[...] += y[tm:n_rows]

    pl.when(first_col)(lambda: start(x_ref, slice(0, tm), o_ref))
    if dual:
        n_all = h_scr.shape[0]
        pl.when(jnp.logical_and(first_col, first_tile))(lambda: start(xs_ref, slice(tm, n_all), os_ref))
        pl.when(first_tile)(lambda: accumulate(n_all))
        pl.when(jnp.logical_not(first_tile))(lambda: accumulate(tm))
    else:
        accumulate(tm)
    if final_norm:
        @pl.when(last_col)
        def _():
            o_ref[...] = _rms(o_ref[...], gf_ref[...])

        if dual:
            @pl.when(jnp.logical_and(last_col, first_tile))
            def _():
                os_ref[...] = _rms(os_ref[...], gf_ref[...])


def _ffn(x, g, w_in, w_out, layer, *, small=None, final_g=None, tf=256, tm_cap=1024):
    m, d = x.shape
    f = w_out.shape[1]
    tm = _row_tile(m, tm_cap)
    nf = f // tf
    dual = small is not None
    ms = small.shape[0] if dual else 0
    in_specs = [
        pl.BlockSpec((tm, d), lambda i, j: (i, 0)),
        pl.BlockSpec((1, d), lambda i, j: (0, 0)),
        pl.BlockSpec((None, d, tf), lambda i, j: (layer, 0, j)),
        pl.BlockSpec((None, d, tf), lambda i, j: (layer, 0, j + nf)),
        pl.BlockSpec((None, tf, d), lambda i, j: (layer, j, 0)),
    ]
    args = [x, g.reshape(1, d), w_in, w_in, w_out]
    if final_g is not None:
        in_specs.append(pl.BlockSpec((1, d), lambda i, j: (0, 0)))
        args.append(final_g.reshape(1, d))
    out_specs = [pl.BlockSpec((tm, d), lambda i, j: (i, 0))]
    out_shape = [jax.ShapeDtypeStruct((m, d), F32)]
    if dual:
        in_specs.append(pl.BlockSpec((ms, d), lambda i, j: (0, 0)))
        args.append(small)
        out_specs.append(pl.BlockSpec((ms, d), lambda i, j: (0, 0)))
        out_shape.append(jax.ShapeDtypeStruct((ms, d), F32))
    vmem = (4 * (tm + ms) * d * 4 + (tm + ms) * d * 2 + 3 * 2 * d * tf * 4 + 3 * d * tf * 2
            + 3 * (tm + ms) * tf * 4)
    outs = pl.pallas_call(
        functools.partial(_ffn_body, dual=dual, final_norm=final_g is not None),
        grid=(m // tm, nf),
        in_specs=in_specs,
        out_specs=out_specs,
        out_shape=out_shape,
        scratch_shapes=[pltpu.VMEM((tm + ms, d), BF16)],
        compiler_params=_params(vmem, 2),
        name="ffn",
    )(*args)
    return tuple(outs) if dual else outs[0]


def _xattn_body(x_ref, g_ref, wq_ref, wo_ref, mk_ref, mv_ref, o_ref, wq_s, wo_s, mk_s, mv_s, *, tiles_per_batch):
    @pl.when(pl.program_id(0) == 0)
    def _():
        wq_s[...] = wq_ref[...].astype(BF16)
        wo_s[...] = wo_ref[...].astype(BF16)

    @pl.when(pl.program_id(0) % tiles_per_batch == 0)
    def _():
        for hh in range(X_HEADS):
            mk_s[hh] = mk_ref[:, hh, :].astype(BF16)
            mv_s[hh] = mv_ref[:, hh, :].astype(BF16)

    x = x_ref[...]
    h = _rms(x, g_ref[...]).astype(BF16)
    q = _dot(h, wq_s[...])
    heads = []
    for hh in range(X_HEADS):
        sl = slice(hh * X_HEAD_DIM, (hh + 1) * X_HEAD_DIM)
        s = lax.dot_general(q[:, sl].astype(BF16), mk_s[hh], NT_DIMS, preferred_element_type=F32)
        s = s * X_HEAD_DIM ** -0.5
        e = jnp.exp(s - jnp.max(s, axis=-1, keepdims=True))
        a = e / jnp.sum(e, axis=-1, keepdims=True)
        heads.append(_dot(a.astype(BF16), mv_s[hh]))
    o = jnp.concatenate(heads, axis=1).astype(BF16)
    o_ref[...] = x + _dot(o, wo_s[...])


def _xattn(x, g, w_q, w_o, mem_k, mem_v, layer, rows_per_batch, *, tm_cap=512):
    m, d = x.shape
    xw = w_q.shape[-1]
    n_mem = mem_k.shape[2]
    tm = _row_tile(rows_per_batch, tm_cap)
    tiles_per_batch = rows_per_batch // tm
    vmem = 4 * tm * d * 4 + 2 * 2 * d * xw * 4 + 2 * d * xw * 2 + 8 * n_mem * xw * 4 + 6 * tm * xw * 4
    mem_spec = pl.BlockSpec((None, None, n_mem, X_HEADS, X_HEAD_DIM),
                            lambda i: (layer, i // tiles_per_batch, 0, 0, 0))
    mem_scratch = pltpu.VMEM((X_HEADS, n_mem, X_HEAD_DIM), BF16)
    return pl.pallas_call(
        functools.partial(_xattn_body, tiles_per_batch=tiles_per_batch),
        grid=(m // tm,),
        in_specs=[
            pl.BlockSpec((tm, d), lambda i: (i, 0)),
            pl.BlockSpec((1, d), lambda i: (0, 0)),
            pl.BlockSpec((None, d, xw), lambda i: (layer, 0, 0)),
            pl.BlockSpec((None, xw, d), lambda i: (layer, 0, 0)),
            mem_spec,
            mem_spec,
        ],
        out_specs=pl.BlockSpec((tm, d), lambda i: (i, 0)),
        out_shape=jax.ShapeDtypeStruct((m, d), F32),
        scratch_shapes=[pltpu.VMEM((d, xw), BF16), pltpu.VMEM((xw, d), BF16), mem_scratch, mem_scratch],
        compiler_params=_params(vmem, 1),
        name="xattn",
    )(x, g.reshape(1, d), w_q, w_o, mem_k, mem_v)


def _memkv_body(m_ref, g_ref, wk_ref, wv_ref, k_ref, v_ref):
    mn = _rms(m_ref[...], g_ref[...]).astype(BF16)
    k = _dot(mn, wk_ref[...].astype(BF16))
    v = _dot(mn, wv_ref[...].astype(BF16))
    for hh in range(X_HEADS):
        sl = slice(hh * X_HEAD_DIM, (hh + 1) * X_HEAD_DIM)
        k_ref[:, hh, :] = k[:, sl]
        v_ref[:, hh, :] = v[:, sl]


def _memkv(mem, norm_mem, w_xk, w_xv):
    depth, d, xw = w_xk.shape
    m = mem.shape[0]
    out = jax.ShapeDtypeStruct((depth, m, X_HEADS, X_HEAD_DIM), F32)
    vmem = 2 * m * d * 4 + m * d * 2 + 4 * d * xw * 4 + 2 * d * xw * 2 + 6 * m * xw * 4
    return pl.pallas_call(
        _memkv_body,
        grid=(depth,),
        in_specs=[
            pl.BlockSpec((m, d), lambda l: (0, 0)),
            pl.BlockSpec((None, 1, d), lambda l: (l, 0, 0)),
            pl.BlockSpec((None, d, xw), lambda l: (l, 0, 0)),
            pl.BlockSpec((None, d, xw), lambda l: (l, 0, 0)),
        ],
        out_specs=[pl.BlockSpec((None, m, X_HEADS, X_HEAD_DIM), lambda l: (l, 0, 0, 0))] * 2,
        out_shape=[out, out],
        compiler_params=_params(vmem, 1),
        name="memkv",
    )(mem, norm_mem.reshape(depth, 1, d), w_xk, w_xv)


def _rmsnorm_body(x_ref, g_ref, o_ref):
    o_ref[...] = _rms(x_ref[...], g_ref[...])


def _rmsnorm(x, g, *, tm_cap=512):
    m, d = x.shape
    tm = _row_tile(m, tm_cap)
    return pl.pallas_call(
        _rmsnorm_body,
        grid=(m // tm,),
        in_specs=[pl.BlockSpec((tm, d), lambda i: (i, 0)), pl.BlockSpec((1, d), lambda i: (0, 0))],
        out_specs=pl.BlockSpec((tm, d), lambda i: (i, 0)),
        out_shape=jax.ShapeDtypeStruct((m, d), F32),
        compiler_params=_params(6 * tm * d * 4, 1),
        name="rmsnorm",
    )(x, g.reshape(1, d))


def _suffix_matrix(n):
    r = lax.broadcasted_iota(jnp.int32, (n, n), 0)
    c = lax.broadcasted_iota(jnp.int32, (n, n), 1)
    return jnp.where(r >= c, 1.0, 0.0).astype(BF16)


def _sb_suffix(z, mask, u_mat):
    nz = -z
    lk = jnp.minimum(nz, 0.0) - jnp.log2(1.0 + jnp.exp2(jnp.minimum(z, nz)))
    if mask is not None:
        lk = jnp.where(mask, lk, 0.0)
    hi, lo = _split_hi_lo(lk)
    return _dot(hi, u_mat) + _dot(lo, u_mat)


def _sb_weights(z, incl, mask, acc):
    w = jnp.exp2(z + incl + acc)
    return w if mask is None else jnp.where(mask, w, 0.0)


def _sb_tile(z, mask, acc, u_mat):
    incl = _sb_suffix(z, mask, u_mat)
    return _sb_weights(z, incl, mask, acc), acc + incl[:, 0:1]


def _sbp_body(bias_ref, q_ref, k_ref, v_ref, o_ref, *, tq, tk, scale):
    h = pl.program_id(1)
    i = pl.program_id(2)
    bias = bias_ref[h] * LOG2E
    q = (q_ref[...] * (scale * LOG2E)).astype(BF16)
    u_mat = _suffix_matrix(tk)
    n_diag = tq // tk
    nk = (i + 1) * n_diag

    def logits(q_rows, start):
        kb = k_ref[pl.ds(start, tk), :].astype(BF16)
        return lax.dot_general(q_rows, kb, NT_DIMS, preferred_element_type=F32) + bias

    def sweep(tiles, acc, out):
        incls = [_sb_suffix(z, mask, u_mat) for z, mask, _ in tiles]
        for (z, mask, start), incl in zip(tiles, incls):
            w = _sb_weights(z, incl, mask, acc).astype(BF16)
            acc = acc + incl[:, 0:1]
            out = out + _dot(w, v_ref[pl.ds(start, tk), :].astype(BF16))
        return acc, out

    def diagonal_group():
        base = i * tq
        tri = lax.broadcasted_iota(jnp.int32, (tk, tk), 1) < lax.broadcasted_iota(jnp.int32, (tk, tk), 0)
        starts = [pl.multiple_of(base + c * tk, tk) for c in range(n_diag)]
        zs = {(s, c): logits(q[s * tk:(s + 1) * tk], starts[c]) for s in range(n_diag) for c in range(s + 1)}
        accs, outs = [], []
        for s in range(n_diag):
            tiles = [(zs[s, c], tri if c == s else None, starts[c]) for c in range(s, -1, -1)]
            acc, out = sweep(tiles, jnp.zeros((tk, 1), F32), jnp.zeros((tk, dh), F32))
            accs.append(acc)
            outs.append(out)
        return jnp.concatenate(accs, axis=0), jnp.concatenate(outs, axis=0)

    def unmasked_group(m, carry):
        starts = [pl.multiple_of((nk - 1 - (n_diag * m + r)) * tk, tk) for r in range(n_diag)]
        return sweep([(logits(q, s), None, s) for s in starts], *carry)

    dh = q_ref.shape[1]
    _, out = lax.fori_loop(1, i + 1, unmasked_group, diagonal_group())
    o_ref[...] = out.astype(o_ref.dtype)


def _sb_attend_prompt(qkv, bias, nb, t, *, tq=2048):
    m, d3 = qkv.shape
    d = d3 // 3
    dh = d // SB_HEADS
    tq = min(tq, t)
    tk = min(SB_KEY_TILE, tq)
    nq = t // tq
    vmem = 4 * t * dh * 4 + 4 * tq * dh * 4 + 16 * tq * tk * 4
    return pl.pallas_call(
        functools.partial(_sbp_body, tq=tq, tk=tk, scale=dh ** -0.5),
        grid=(nb, SB_HEADS, nq),
        in_specs=[
            pl.BlockSpec(memory_space=pltpu.SMEM),
            pl.BlockSpec((tq, dh), lambda b, h, i: (b * nq + i, h)),
            pl.BlockSpec((t, dh), lambda b, h, i: (b, SB_HEADS + h)),
            pl.BlockSpec((t, dh), lambda b, h, i: (b, 2 * SB_HEADS + h)),
        ],
        out_specs=pl.BlockSpec((tq, dh), lambda b, h, i: (b * nq + i, h)),
        out_shape=jax.ShapeDtypeStruct((m, d), _mxu_feed_dtype(tq)),
        compiler_params=_params(vmem, 3),
        name="sb_prompt",
    )(bias, qkv, qkv, qkv)


def _sbs_body(pt_ref, bias_ref, qkv_ref, *refs, t, dh, page, npp, scale):
    del pt_ref
    k_refs, v_refs = refs[:npp], refs[npp:2 * npp]
    o_ref, out_s, acc_s = refs[2 * npp:]
    s = pl.program_id(1)
    d = SB_HEADS * dh
    rows = SB_HEADS * t
    u_mat = _suffix_matrix(page)

    def attend(pages, mask, acc, outs):
        qs = [(qkv_ref[:, h * dh:(h + 1) * dh] * (scale * LOG2E)).astype(BF16) for h in range(SB_HEADS)]
        zs = [jnp.concatenate([lax.dot_general(qs[h], get_k(h), NT_DIMS, preferred_element_type=F32)
                               + bias_ref[h] * LOG2E for h in range(SB_HEADS)], axis=0)
              for get_k, _ in pages]
        incls = [_sb_suffix(z, mask, u_mat) for z in zs]
        ws = []
        for z, incl in zip(zs, incls):
            ws.append(_sb_weights(z, incl, mask, acc))
            acc = acc + incl[:, 0:1]
        for w, (_, get_v) in zip(ws, pages):
            outs = [outs[h] + _dot(w[h * t:(h + 1) * t, :].astype(BF16), get_v(h)) for h in range(SB_HEADS)]
        return acc, outs

    def store(acc, outs):
        acc_s[...] = acc
        for h in range(SB_HEADS):
            out_s[h * t:(h + 1) * t, :] = outs[h]

    @pl.when(s == 0)
    def _():
        pad = jnp.zeros((page - t, dh), F32)
        key = lax.broadcasted_iota(jnp.int32, (rows, page), 1)
        qry = lax.broadcasted_iota(jnp.int32, (rows, page), 0) % t
        new_tokens = (
            lambda h: jnp.concatenate([qkv_ref[:, d + h * dh:d + (h + 1) * dh], pad], axis=0).astype(BF16),
            lambda h: jnp.concatenate([qkv_ref[:, 2 * d + h * dh:2 * d + (h + 1) * dh], pad], axis=0).astype(BF16))
        acc, outs = attend([new_tokens], key < qry, jnp.zeros((rows, 1), F32),
                           [jnp.zeros((t, dh), F32)] * SB_HEADS)
        store(acc, outs)

    def cached_page(j):
        return (lambda h: k_refs[j][pl.ds(h, page, stride=SB_HEADS), :].astype(BF16),
                lambda h: v_refs[j][pl.ds(h, page, stride=SB_HEADS), :].astype(BF16))

    acc, outs = attend([cached_page(j) for j in range(npp)], None, acc_s[...],
                       [out_s[h * t:(h + 1) * t, :] for h in range(SB_HEADS)])
    store(acc, outs)

    @pl.when(s == pl.num_programs(1) - 1)
    def _():
        for h in range(SB_HEADS):
            o_ref[:, h * dh:(h + 1) * dh] = outs[h]


def _sb_attend_sample(qkv, bias, cache_k, cache_v, page_table, nb, t, *, npp=8):
    m, d3 = qkv.shape
    d = d3 // 3
    dh = d // SB_HEADS
    n_phys, page = cache_k.shape[0], cache_k.shape[1]
    n_pages = page_table.shape[1]
    while n_pages % npp:
        npp //= 2
    ck = cache_k.reshape(n_phys, page * SB_HEADS, dh)
    cv = cache_v.reshape(n_phys, page * SB_HEADS, dh)

    def cache_spec(j):
        return pl.BlockSpec((None, page * SB_HEADS, dh),
                            lambda b, s, pt: (pt[b, n_pages - 1 - (s * npp + j)], 0, 0))

    vmem = 4 * npp * page * d * 4 + 2 * t * d3 * 4 + 24 * npp * SB_HEADS * t * page * 4
    return pl.pallas_call(
        functools.partial(_sbs_body, t=t, dh=dh, page=page, npp=npp, scale=dh ** -0.5),
        grid_spec=pltpu.PrefetchScalarGridSpec(
            num_scalar_prefetch=1,
            grid=(nb, n_pages // npp),
            in_specs=[pl.BlockSpec(memory_space=pltpu.SMEM), pl.BlockSpec((t, d3), lambda b, s, pt: (b, 0))]
            + [cache_spec(j) for j in range(npp)] * 2,
            out_specs=pl.BlockSpec((t, d), lambda b, s, pt: (b, 0)),
            scratch_shapes=[pltpu.VMEM((SB_HEADS * t, dh), F32), pltpu.VMEM((SB_HEADS * t, 1), F32)],
        ),
        out_shape=jax.ShapeDtypeStruct((m, d), F32),
        compiler_params=_params(vmem, 2),
        name="sb_sample",
    )(page_table, bias, qkv, *([ck] * npp), *([cv] * npp))


def _cmul(ar, ai, br, bi):
    return ar * br - ai * bi, ar * bi + ai * br


def _gelu_tanh(x):
    return 0.5 * x * (1.0 + jnp.tanh(math.sqrt(2.0 / math.pi) * (x + 0.044715 * (x * x * x))))


def _s5_body(h_ref, ar_ref, ai_ref, br_ref, bi_ref, cr_ref, ci_ref, d_ref, s0r_ref, s0i_ref,
             y_ref, sr_ref, si_ref, xr_s, xi_s, pr_s, pi_s, cr_s, ci_s, *, nbk):
    t = pl.program_id(2)
    rows8, n_state = pr_s.shape
    row = lax.broadcasted_iota(jnp.int32, (rows8, n_state), 0)

    def last_step_to_all_rows(x):
        have = nbk
        while have < rows8:
            x = jnp.where(row >= rows8 - have, x, pltpu.roll(x, rows8 - have, 0))
            have *= 2
        return x

    @pl.when(t == 0)
    def _():
        ar, ai = ar_ref[...], ai_ref[...]
        pr, pi = ar, ai
        for r in range(rows8):
            if r and r % nbk == 0:
                pr, pi = _cmul(pr, pi, ar, ai)
            pr_s[r:r + 1, :] = pr
            pi_s[r:r + 1, :] = pi
        cr_s[...] = jnp.zeros_like(cr_s)
        ci_s[...] = jnp.zeros_like(ci_s)
        cr_s[rows8 - nbk:, :] = s0r_ref[...]
        ci_s[rows8 - nbk:, :] = s0i_ref[...]
        cr_s[...] = last_step_to_all_rows(cr_s[...])
        ci_s[...] = last_step_to_all_rows(ci_s[...])

    tt = h_ref.shape[1]
    n_slab = xr_s.shape[0]
    lanes = [slice(s * V7X_LANES, (s + 1) * V7X_LANES) for s in range(n_slab)]

    def batch_rows(b):
        return pl.ds(b, tt, stride=nbk) if nbk > 1 else pl.ds(0, tt)

    b_r, b_i = br_ref[...].astype(BF16), bi_ref[...].astype(BF16)
    for b in range(nbk):
        u = h_ref[b].astype(BF16)
        bu_r, bu_i = _dot(u, b_r), _dot(u, b_i)
        for s in range(n_slab):
            xr_s[s, batch_rows(b), :] = bu_r[:, lanes[s]]
            xi_s[s, batch_rows(b), :] = bu_i[:, lanes[s]]
    pw_r, pw_i = pr_s[...], pi_s[...]

    def tile(blk, carry):
        c_r, c_i = carry
        rows = pl.ds(pl.multiple_of(blk * rows8, rows8), rows8)
        x_r = jnp.concatenate([xr_s[s, rows, :] for s in range(n_slab)], axis=1)
        x_i = jnp.concatenate([xi_s[s, rows, :] for s in range(n_slab)], axis=1)
        dist = 1
        while dist * nbk < rows8:
            lo = (dist - 1) * nbk
            a_r, a_i = pw_r[lo:lo + 1, :], pw_i[lo:lo + 1, :]
            keep = row >= dist * nbk
            sh_r = jnp.where(keep, pltpu.roll(x_r, dist * nbk, 0), 0.0)
            sh_i = jnp.where(keep, pltpu.roll(x_i, dist * nbk, 0), 0.0)
            d_r, d_i = _cmul(a_r, a_i, sh_r, sh_i)
            x_r, x_i = x_r + d_r, x_i + d_i
            dist *= 2
        d_r, d_i = _cmul(pw_r, pw_i, c_r, c_i)
        x_r, x_i = x_r + d_r, x_i + d_i
        for s in range(n_slab):
            xr_s[s, rows, :] = x_r[:, lanes[s]]
            xi_s[s, rows, :] = x_i[:, lanes[s]]
        return last_step_to_all_rows(x_r), last_step_to_all_rows(x_i)

    c_r, c_i = lax.fori_loop(0, xr_s.shape[1] // rows8, tile, (cr_s[...], ci_s[...]))
    cr_s[...] = c_r
    ci_s[...] = c_i

    c_re, c_im = cr_ref[...].astype(BF16), ci_ref[...].astype(BF16)
    for b in range(nbk):
        st_r = jnp.concatenate([xr_s[s, batch_rows(b), :] for s in range(n_slab)], axis=1).astype(BF16)
        st_i = jnp.concatenate([xi_s[s, batch_rows(b), :] for s in range(n_slab)], axis=1).astype(BF16)
        y = _dot(st_r, c_re) - _dot(st_i, c_im)
        y_ref[b] = _gelu_tanh(y + d_ref[...] * h_ref[b]).astype(y_ref.dtype)

    @pl.when(t == pl.num_programs(2) - 1)
    def _():
        sr_ref[...] = c_r[rows8 - nbk:, :]
        si_ref[...] = c_i[rows8 - nbk:, :]


def _s5_discretize(a_re, a_im, log_dt, b_re, b_im, c_re, c_im):
    g, p = a_re.shape
    gb = S5_CH_BLOCK // S5_GROUP_CH
    nblk = g // gb
    dt = jnp.exp(log_dt)[:, None]
    mag = jnp.exp(a_re * dt)
    abar_r, abar_i = mag * jnp.cos(a_im * dt), mag * jnp.sin(a_im * dt)
    xr, xi = abar_r - 1.0, abar_i
    den = a_re * a_re + a_im * a_im
    coef_r = (xr * a_re + xi * a_im) / den
    coef_i = (xi * a_re - xr * a_im) / den
    bbar_r = coef_r[..., None] * b_re - coef_i[..., None] * b_im
    bbar_i = coef_r[..., None] * b_im + coef_i[..., None] * b_re
    eye = jnp.eye(gb, dtype=F32)

    def b_big(bb):
        blk = bb.reshape(nblk, gb, p, S5_GROUP_CH).transpose(0, 1, 3, 2)
        return blk[:, :, :, None, :] * eye[None, :, None, :, None]

    def c_big(cc):
        blk = cc.reshape(nblk, gb, S5_GROUP_CH, p).transpose(0, 1, 3, 2)
        return blk[:, :, :, None, :] * eye[None, :, None, :, None]

    n_state = gb * p
    return (abar_r.reshape(1, g * p), abar_i.reshape(1, g * p),
            b_big(bbar_r).reshape(nblk, S5_CH_BLOCK, n_state), b_big(bbar_i).reshape(nblk, S5_CH_BLOCK, n_state),
            c_big(c_re).reshape(nblk, n_state, S5_CH_BLOCK), c_big(c_im).reshape(nblk, n_state, S5_CH_BLOCK))


def _s5_core(h, disc, d_skip, s0r, s0i, nb, t, *, tt_cap=512):
    m, d = h.shape
    nbk = math.gcd(nb, V7X_SUBLANES)
    abar_r, abar_i, bbr, bbi, ccr, cci = disc
    nblk, cb, n_state = bbr.shape
    tt = _row_tile(t, tt_cap)
    nt = t // tt
    ng = nb // nbk
    state = jax.ShapeDtypeStruct((ng, nbk, nblk * n_state), F32)
    lane_spec = pl.BlockSpec((1, n_state), lambda b, c, i: (0, c))
    state_spec = pl.BlockSpec((None, nbk, n_state), lambda b, c, i: (b, 0, c))
    b_spec = pl.BlockSpec((None, cb, n_state), lambda b, c, i: (c, 0, 0))
    c_spec = pl.BlockSpec((None, n_state, cb), lambda b, c, i: (c, 0, 0))
    row_spec = pl.BlockSpec((nbk, tt, cb), lambda b, c, i: (b, i, c))
    vmem = 4 * tt * nbk * cb * 4 + 16 * cb * n_state * 4 + 8 * tt * nbk * n_state * 4
    lane_tile = pltpu.VMEM((V7X_SUBLANES, n_state), F32)
    slabs = pltpu.VMEM((n_state // V7X_LANES, tt * nbk, V7X_LANES), F32)
    y, sr, si = pl.pallas_call(
        functools.partial(_s5_body, nbk=nbk),
        grid=(ng, nblk, nt),
        in_specs=[row_spec, lane_spec, lane_spec, b_spec, b_spec, c_spec, c_spec,
                  pl.BlockSpec((1, cb), lambda b, c, i: (0, c)), state_spec, state_spec],
        out_specs=[row_spec, state_spec, state_spec],
        out_shape=[jax.ShapeDtypeStruct((nb, t, d), _mxu_feed_dtype(tt)), state, state],
        scratch_shapes=[slabs, slabs, lane_tile, lane_tile, lane_tile, lane_tile],
        compiler_params=_params(vmem, 3),
        name="s5",
    )(h.reshape(nb, t, d), abar_r, abar_i, bbr, bbi, ccr, cci, d_skip.reshape(1, d),
      s0r.reshape(ng, nbk, -1), s0i.reshape(ng, nbk, -1))
    return y.reshape(m, d), sr.reshape(nb, -1), si.reshape(nb, -1)


def _conv_body(b_ref, c_ref, v_ref, w_ref, p_ref, y_ref, s_ref):
    z = c_ref[...] * v_ref[...]
    t = z.shape[0]
    row = lax.broadcasted_iota(jnp.int32, z.shape, 0)
    p0, p1 = p_ref[0:1, :], p_ref[1:2, :]
    z1 = jnp.where(row >= 1, pltpu.roll(z, 1, 0), p1)
    z2 = jnp.where(row >= 2, pltpu.roll(z, 2, 0), jnp.where(row == 0, p0, p1))
    y = w_ref[0:1, :] * z2 + w_ref[1:2, :] * z1 + w_ref[2:3, :] * z
    y_ref[...] = (b_ref[...] * y).astype(y_ref.dtype)
    s_ref[...] = z[t - (CONV_W - 1):, :]


def _conv_core(bcv, w_conv, prev, nb, t, *, tc=512):
    m, d3 = bcv.shape
    d = d3 // 3
    nc = d // tc
    vmem = 12 * t * tc * 4
    return pl.pallas_call(
        _conv_body,
        grid=(nb, nc),
        in_specs=[
            pl.BlockSpec((t, tc), lambda b, j: (b, j)),
            pl.BlockSpec((t, tc), lambda b, j: (b, nc + j)),
            pl.BlockSpec((t, tc), lambda b, j: (b, 2 * nc + j)),
            pl.BlockSpec((CONV_W, tc), lambda b, j: (0, j)),
            pl.BlockSpec((None, CONV_W - 1, tc), lambda b, j: (b, 0, j)),
        ],
        out_specs=[pl.BlockSpec((t, tc), lambda b, j: (b, j)),
                   pl.BlockSpec((None, CONV_W - 1, tc), lambda b, j: (b, 0, j))],
        out_shape=[jax.ShapeDtypeStruct((m, d), _mxu_feed_dtype(t)), jax.ShapeDtypeStruct((nb, CONV_W - 1, d), F32)],
        compiler_params=_params(vmem, 2),
        name="conv",
    )(bcv, bcv, bcv, w_conv, prev)


def _gla_gate_body(x_ref, g_ref, w1_ref, w2_ref, b_ref, o_ref):
    h = _rms(x_ref[...], g_ref[...]).astype(BF16)
    low = _dot(h, w1_ref[...].astype(BF16))
    y = _dot(low.astype(BF16), w2_ref[...].astype(BF16)) + b_ref[...]
    o_ref[...] = -_softplus(-y) / GLA_TAU


def _gla_gate(x, g, w1, w2, b, *, tm_cap=512):
    m, d = x.shape
    r = w1.shape[1]
    gk = w2.shape[1]
    rp = V7X_LANES
    w1p = jnp.pad(w1, ((0, 0), (0, rp - r)))
    w2p = jnp.pad(w2, ((0, rp - r), (0, 0)))
    tm = _row_tile(m, tm_cap)
    vmem = 3 * tm * d * 4 + 4 * d * rp * 4 + 4 * rp * gk * 4 + 6 * tm * gk * 4
    return pl.pallas_call(
        _gla_gate_body,
        grid=(m // tm,),
        in_specs=[
            pl.BlockSpec((tm, d), lambda i: (i, 0)),
            pl.BlockSpec((1, d), lambda i: (0, 0)),
            pl.BlockSpec((d, rp), lambda i: (0, 0)),
            pl.BlockSpec((rp, gk), lambda i: (0, 0)),
            pl.BlockSpec((1, gk), lambda i: (0, 0)),
        ],
        out_specs=pl.BlockSpec((tm, gk), lambda i: (i, 0)),
        out_shape=jax.ShapeDtypeStruct((m, gk), F32),
        compiler_params=_params(vmem, 1),
        name="gla_gate",
    )(x, g.reshape(1, d), w1p, w2p, b.reshape(1, gk))


def _cumsum_rows(x):
    n = x.shape[0]
    row = lax.broadcasted_iota(jnp.int32, x.shape, 0)
    dist = 1
    while dist < n:
        x = x + jnp.where(row >= dist, pltpu.roll(x, dist, 0), 0.0)
        dist *= 2
    return x


def _gla_chunk(q, k, v, glog, st):
    c, dk = q.shape
    b = _cumsum_rows(glog)
    o = lax.dot_general((q * jnp.exp(b)).astype(BF16), st.astype(BF16), NT_DIMS, preferred_element_type=F32)
    rowc = lax.broadcasted_iota(jnp.int32, (c, 1), 0)
    lane = lax.broadcasted_iota(jnp.int32, (1, c), 1)
    rsub = lax.broadcasted_iota(jnp.int32, (GLA_SUB, 1), 0)
    att_rows = []
    for blk in range(c // GLA_SUB):
        lo = blk * GLA_SUB
        b_i = b[lo:lo + GLA_SUB]
        q_i = q[lo:lo + GLA_SUB]
        if blk > 0:
            b_0 = b[lo - 1:lo]
            q_off = (q_i * jnp.exp(b_i - b_0)).astype(BF16)
            k_off = (k * jnp.exp(jnp.where(rowc < lo, b_0 - b, NEG_INF))).astype(BF16)
            att = lax.dot_general(q_off, k_off, NT_DIMS, preferred_element_type=F32)
        else:
            att = jnp.zeros((GLA_SUB, c), F32)
        for jj in range(GLA_SUB):
            j = lo + jj
            dec = jnp.exp(jnp.where(rsub >= jj, b_i - b[j:j + 1], NEG_INF))
            col = jnp.sum(q_i * k[j:j + 1] * dec, axis=1, keepdims=True)
            att = att + jnp.where(lane == j, col, 0.0)
        att_rows.append(att)
    att = att_rows[0] if len(att_rows) == 1 else jnp.concatenate(att_rows, axis=0)
    vb = v.astype(BF16)
    o = o + _dot(att.astype(BF16), vb)
    b_last = b[c - 1:c]
    k_dec = (k * jnp.exp(b_last - b)).astype(BF16)
    st = st * jnp.exp(b_last) + lax.dot_general(vb, k_dec, TN_DIMS, preferred_element_type=F32)
    return o, st


def _gla_body(*refs, chunk, has_prev, scale):
    refs = list(refs)
    q_ref, k_ref, v_ref, r_ref, gl_ref, ng_ref = refs[:6]
    s0_ref = refs[6] if has_prev else None
    o_ref, s_ref, st_s = refs[-3:]
    t = pl.program_id(1)
    n_heads, dv, dk = st_s.shape

    @pl.when(t == 0)
    def _():
        for h in range(n_heads):
            st_s[h] = s0_ref[h].T if has_prev else jnp.zeros((dv, dk), F32)

    def step(ci, carry):
        rows = pl.ds(pl.multiple_of(ci * chunk, chunk), chunk)
        for h in range(n_heads):
            ck, cv = slice(h * dk, (h + 1) * dk), slice(h * dv, (h + 1) * dv)
            o, st = _gla_chunk(q_ref[rows, ck] * scale, k_ref[rows, ck], v_ref[rows, cv], gl_ref[rows, ck], st_s[h])
            st_s[h] = st
            o = o * lax.rsqrt(jnp.mean(o * o, axis=-1, keepdims=True) + EPS) * ng_ref[...]
            o_ref[rows, cv] = (o * jax.nn.silu(r_ref[rows, cv])).astype(o_ref.dtype)
        return carry

    lax.fori_loop(0, q_ref.shape[0] // chunk, step, 0)

    @pl.when(t == pl.num_programs(1) - 1)
    def _():
        for h in range(n_heads):
            s_ref[h] = st_s[h].T


def _gla_core(qkvr, glog, norm_g, prev, nb, t, *, tt_cap=512):
    m = qkvr.shape[0]
    gk = glog.shape[1]
    dk = gk // GLA_HEADS
    d = (qkvr.shape[1] - 2 * gk) // 2
    dv = d // GLA_HEADS
    chunk = GLA_CHUNK if t % GLA_CHUNK == 0 else t
    tt = _row_tile(t, tt_cap)
    nt = t // tt
    in_specs = [
        pl.BlockSpec((tt, gk), lambda b, i: (b * nt + i, 0)),
        pl.BlockSpec((tt, gk), lambda b, i: (b * nt + i, 1)),
        pl.BlockSpec((tt, d), lambda b, i: (b * nt + i, 2 * gk // d)),
        pl.BlockSpec((tt, d), lambda b, i: (b * nt + i, 2 * gk // d + 1)),
        pl.BlockSpec((tt, gk), lambda b, i: (b * nt + i, 0)),
        pl.BlockSpec((1, dv), lambda b, i: (0, 0)),
    ]
    args = [qkvr, qkvr, qkvr, qkvr, glog, norm_g.reshape(1, dv)]
    state_spec = pl.BlockSpec((None, GLA_HEADS, dk, dv), lambda b, i: (b, 0, 0, 0))
    if prev is not None:
        in_specs.append(state_spec)
        args.append(prev)
    vmem = 4 * tt * (3 * gk + 3 * d) * 4 + 7 * GLA_HEADS * dk * dv * 4 + 64 * GLA_HEADS * chunk * dv * 4
    return pl.pallas_call(
        functools.partial(_gla_body, chunk=chunk, has_prev=prev is not None, scale=dk ** -0.5),
        grid=(nb, nt),
        in_specs=in_specs,
        out_specs=[pl.BlockSpec((tt, d), lambda b, i: (b * nt + i, 0)), state_spec],
        out_shape=[jax.ShapeDtypeStruct((m, d), _mxu_feed_dtype(chunk)),
                   jax.ShapeDtypeStruct((nb, GLA_HEADS, dk, dv), F32)],
        scratch_shapes=[pltpu.VMEM((GLA_HEADS, dv, dk), F32)],
        compiler_params=_params(vmem, 2),
        name="gla",
    )(*args)


def _trunk(xp, xs, prompt_dims, sample_dims, mem_p, mem_s, p, s5_disc, sample):
    (nbp, tp), (nbs, ts) = prompt_dims, sample_dims
    d = xp.shape[1]
    depth = p["norm_mix"].shape[0]
    new_p, new_s = {}, {}
    for i in range(depth):
        g_mix = p["norm_mix"][i]
        kind = i % 4
        if kind == 0:
            qkv_p, qkv_s = _mm(xp, p["w_sb_qkv"], g=g_mix, small=xs)
            o_p = _sb_attend_prompt(qkv_p, p["sb_bias"], nbp, tp)
            o_s = _sb_attend_sample(qkv_s, p["sb_bias"], sample["cache_sb_k"], sample["cache_sb_v"],
                                    sample["page_table"], nbs, ts)
            xp, xs = _mm(o_p, p["w_sb_out"], res=xp, small=o_s, small_res=xs)
            for new, qkv, nb, t in ((new_p, qkv_p, nbp, tp), (new_s, qkv_s, nbs, ts)):
                shp = (nb, t, SB_HEADS, d // SB_HEADS)
                new["sb"] = (qkv[:, d:2 * d].reshape(shp), qkv[:, 2 * d:].reshape(shp))
        elif kind == 1:
            n_state = d // S5_GROUP_CH * S5_STATE
            zeros = jnp.zeros((nbp, n_state), F32)
            y_p, sr_p, si_p = _s5_core(_rmsnorm(xp, g_mix), s5_disc, p["s5_d"], zeros, zeros, nbp, tp)
            y_s, sr_s, si_s = _s5_core(_rmsnorm(xs, g_mix), s5_disc, p["s5_d"], sample["state_s5_re"],
                                       sample["state_s5_im"], nbs, ts)
            xp, xs = _mm(y_p, p["w_s5_glu"], res=xp, glu=True, small=y_s, small_res=xs)
            for new, sr, si, nb in ((new_p, sr_p, si_p, nbp), (new_s, sr_s, si_s, nbs)):
                sshape = (nb, d // S5_GROUP_CH, S5_STATE)
                new["s5"] = (sr.reshape(sshape), si.reshape(sshape))
        elif kind == 2:
            bcv_p, bcv_s = _mm(xp, p["w_conv_in"], g=g_mix, small=xs)
            yg_p, new_p["conv"] = _conv_core(bcv_p, p["w_conv"], jnp.zeros((nbp, CONV_W - 1, d), F32), nbp, tp)
            yg_s, new_s["conv"] = _conv_core(bcv_s, p["w_conv"], sample["state_conv"], nbs, ts)
            xp, xs = _mm(yg_p, p["w_conv_out"], res=xp, small=yg_s, small_res=xs)
        else:
            qkvr_p, qkvr_s = _mm(xp, p["w_gla_in"], g=g_mix, small=xs)
            gate = (g_mix, p["w_gla_g1"], p["w_gla_g2"], p["b_gla_g"])
            o_p, new_p["gla"] = _gla_core(qkvr_p, _gla_gate(xp, *gate), p["gla_norm"], None, nbp, tp)
            o_s, new_s["gla"] = _gla_core(qkvr_s, _gla_gate(xs, *gate), p["gla_norm"], sample["state_gla"], nbs, ts)
            xp, xs = _mm(o_p, p["w_gla_out"], res=xp, small=o_s, small_res=xs)
        xp = _xattn(xp, p["norm_xattn"][i], p["w_xq"], p["w_xo"], mem_p[0], mem_p[1], i, tp)
        xs = _xattn(xs, p["norm_xattn"][i], p["w_xq"], p["w_xo"], mem_s[0], mem_s[1], i, ts)
        xp, xs = _ffn(xp, p["norm_ffn"][i], p["w_ffn_in"], p["w_ffn_out"], i, small=xs,
                      final_g=p["norm_final"] if i == depth - 1 else None)
    return xp.reshape(nbp, tp, d), xs.reshape(nbs, ts, d), new_p, new_s


def kernel(x_prompt, x_sample, mem_prompt, cache_sb_k, cache_sb_v, page_table, state_s5_re, state_s5_im, state_conv, state_gla, cache_mem_k, cache_mem_v, norm_mix, norm_xattn, norm_mem, norm_ffn, norm_final, w_sb_qkv, w_sb_out, sb_bias, s5_a_re, s5_a_im, s5_log_dt, s5_b_re, s5_b_im, s5_c_re, s5_c_im, s5_d, w_s5_glu, w_conv_in, w_conv, w_conv_out, w_gla_in, w_gla_g1, w_gla_g2, b_gla_g, gla_norm, w_gla_out, w_xq, w_xk, w_xv, w_xo, w_ffn_in, w_ffn_out):
    p = dict(norm_mix=norm_mix, norm_xattn=norm_xattn, norm_ffn=norm_ffn, norm_final=norm_final,
             w_sb_qkv=w_sb_qkv, w_sb_out=w_sb_out, sb_bias=sb_bias, s5_d=s5_d, w_s5_glu=w_s5_glu,
             w_conv_in=w_conv_in, w_conv=w_conv, w_conv_out=w_conv_out,
             w_gla_in=w_gla_in, w_gla_g1=w_gla_g1, w_gla_g2=w_gla_g2, b_gla_g=b_gla_g,
             gla_norm=gla_norm, w_gla_out=w_gla_out, w_xq=w_xq, w_xo=w_xo,
             w_ffn_in=w_ffn_in, w_ffn_out=w_ffn_out)
    nbp, tp, d = x_prompt.shape
    nbs, ts, _ = x_sample.shape
    depth = norm_mix.shape[0]
    n_mem = mem_prompt.shape[1]
    mshape = (depth, nbp, n_mem, X_HEADS, X_HEAD_DIM)
    mem_k_p, mem_v_p = _memkv(mem_prompt.reshape(nbp * n_mem, d), norm_mem, w_xk, w_xv)
    mem_k_p, mem_v_p = mem_k_p.reshape(mshape), mem_v_p.reshape(mshape)
    s5_disc = _s5_discretize(s5_a_re, s5_a_im, s5_log_dt, s5_b_re, s5_b_im, s5_c_re, s5_c_im)

    sample = dict(cache_sb_k=cache_sb_k, cache_sb_v=cache_sb_v, page_table=page_table,
                  state_s5_re=state_s5_re.reshape(nbs, -1), state_s5_im=state_s5_im.reshape(nbs, -1),
                  state_conv=state_conv, state_gla=state_gla)
    y_p, y_s, new_p, new_s = _trunk(x_prompt.reshape(nbp * tp, d), x_sample.reshape(nbs * ts, d),
                                    (nbp, tp), (nbs, ts), (mem_k_p, mem_v_p), (cache_mem_k, cache_mem_v),
                                    p, s5_disc, sample)

    return (y_p, y_s, new_p["sb"][0], new_p["sb"][1], new_s["sb"][0], new_s["sb"][1],
            new_p["s5"][0], new_p["s5"][1], new_s["s5"][0], new_s["s5"][1],
            new_p["conv"], new_s["conv"], new_p["gla"], new_s["gla"],
            mem_k_p, mem_v_p)
```
